```python
import jax, jax.numpy as jnp
from jax import lax
import numpy as np

D_MODEL = 1024
BATCH = 8
SEQ = 2048
DEPTH = 4
DEC_BATCH = 128
DEC_SEQ = 8
PAST_LEN = 16384
PAGE_SIZE = 128

N_MIXERS = 2
N_A_LAYERS = (DEPTH + 1) // 2
N_B_LAYERS = DEPTH // 2
D_RNN = D_MODEL
RG_BLOCKS = 16
RG_BW = D_RNN // RG_BLOCKS
CONV_W = 4
RG_C = 8.0
GLA_HEADS = 4
GLA_DK = D_MODEL // 2
GLA_DV = D_MODEL
GLA_HK = GLA_DK // GLA_HEADS
GLA_HV = GLA_DV // GLA_HEADS
GLA_RANK = 16
GLA_TAU = 16.0
GLA_CHUNK = 64
GLA_IN = 2 * GLA_DK + 2 * GLA_DV + GLA_RANK
N_MEM = 256
XA_HEADS = 4
XA_HD = D_MODEL // XA_HEADS
D_FF = 4 * D_MODEL
EPS = 1e-6

kernel_name = "hybrid_rglru_gla_memxattn_step"


def rmsnorm(x, g):
    xf = x.astype(jnp.float32)
    y = xf * lax.rsqrt(jnp.mean(xf * xf, axis=-1, keepdims=True) + EPS)
    return (y * g.astype(jnp.float32)).astype(x.dtype)


def causal_dwconv(u, buf, w, b):
    T = u.shape[1]
    up = jnp.concatenate([buf.astype(u.dtype), u], axis=1)
    out = b + w[0] * up[:, 0:T]
    for j in range(1, CONV_W):
        out = out + w[j] * up[:, j:j + T]
    return out, up[:, -(CONV_W - 1):]


def rglru(u, h0, w_a, b_a, w_x, b_x, lam):
    B, T, _ = u.shape
    ub = u.reshape(B, T, RG_BLOCKS, RG_BW)
    r = jax.nn.sigmoid(jnp.einsum('bthi,hij->bthj', ub, w_a).reshape(B, T, D_RNN) + b_a)
    i = jax.nn.sigmoid(jnp.einsum('bthi,hij->bthj', ub, w_x).reshape(B, T, D_RNN) + b_x)
    log_a = -RG_C * r.astype(jnp.float32) * jax.nn.softplus(-lam.astype(jnp.float32))
    a = jnp.exp(log_a)
    mult = jnp.sqrt(-jnp.expm1(2.0 * log_a))
    bterm = mult * (i * u).astype(jnp.float32)
    bterm = bterm.at[:, 0].add(a[:, 0] * h0.astype(jnp.float32))

    def combine(left, right):
        a_l, b_l = left
        a_r, b_r = right
        return a_l * a_r, a_r * b_l + b_r

    _, h = lax.associative_scan(combine, (a, bterm), axis=1)
    return h.astype(u.dtype), h[:, -1].astype(h0.dtype)


def rg_block(xn, conv_buf, h0, w_in, conv_w, conv_b, w_a, b_a, w_x, b_x, lam, w_out):
    yx = xn @ w_in
    y_br, x_br = jnp.split(yx, [D_RNN], axis=-1)
    gate = jax.nn.gelu(y_br)
    xc, new_buf = causal_dwconv(x_br, conv_buf, conv_w, conv_b)
    h, h_last = rglru(xc, h0, w_a, b_a, w_x, b_x, lam)
    return (gate * h) @ w_out, new_buf.astype(conv_buf.dtype), h_last


def gla_chunked(q, k, v, log_a, S0, chunk):
    B, T, H, K = q.shape
    V = v.shape[-1]
    n = T // chunk

    def to_chunks(t):
        return jnp.moveaxis(t.reshape(B, n, chunk, H, t.shape[-1]), 1, 0)

    mask = jnp.tril(jnp.ones((chunk, chunk), dtype=bool))

    def step(S, inp):
        qc, kc, vc, lac = inp
        bcum = jnp.cumsum(lac, axis=1)
        g = bcum[:, -1]
        q_in = qc * jnp.exp(bcum)
        k_in = kc * jnp.exp(-bcum)
        att = jnp.where(mask, jnp.einsum('bthk,bshk->bhts', q_in, k_in), 0.0)
        o = jnp.einsum('bhts,bshv->bthv', att, vc) + jnp.einsum('bthk,bhkv->bthv', q_in, S)
        k_end = kc * jnp.exp(g[:, None] - bcum)
        S = jnp.exp(g)[..., None] * S + jnp.einsum('bshk,bshv->bhkv', k_end, vc)
        return S, o

    S, o = lax.scan(step, S0, (to_chunks(q), to_chunks(k), to_chunks(v), to_chunks(log_a)))
    o = jnp.moveaxis(o, 0, 1).reshape(B, T, H, V)
    return o, S


def gla_block(xn, S0, w_in, w_a2, b_a, norm_g, w_out):
    B, T, _ = xn.shape
    proj = xn @ w_in
    q, k, v, g, a_lo = jnp.split(
        proj, [GLA_DK, 2 * GLA_DK, 2 * GLA_DK + GLA_DV, 2 * GLA_DK + 2 * GLA_DV], axis=-1)
    q = q.reshape(B, T, GLA_HEADS, GLA_HK).astype(jnp.float32) * (GLA_HK ** -0.5)
    k = k.reshape(B, T, GLA_HEADS, GLA_HK).astype(jnp.float32)
    v = v.reshape(B, T, GLA_HEADS, GLA_HV).astype(jnp.float32)
    log_a = jax.nn.log_sigmoid((a_lo @ w_a2 + b_a).astype(jnp.float32)) / GLA_TAU
    log_a = log_a.reshape(B, T, GLA_HEADS, GLA_HK)
    chunk = GLA_CHUNK if T % GLA_CHUNK == 0 else T
    o, S = gla_chunked(q, k, v, log_a, S0.astype(jnp.float32), chunk)
    o = rmsnorm(o, norm_g).reshape(B, T, GLA_DV).astype(xn.dtype)
    return (o * jax.nn.silu(g)) @ w_out, S.astype(S0.dtype)


def mem_kv(mem, g, w_k, w_v):
    B, M, _ = mem.shape
    mn = rmsnorm(mem, g)
    return (mn @ w_k).reshape(B, M, XA_HEADS, XA_HD), (mn @ w_v).reshape(B, M, XA_HEADS, XA_HD)


def cross_attn(xn, k, v, w_q, w_o):
    B, T, _ = xn.shape
    q = (xn @ w_q).reshape(B, T, XA_HEADS, XA_HD)
    s = jnp.einsum('bthd,bmhd->bhtm', q, k.astype(q.dtype)).astype(jnp.float32) * (XA_HD ** -0.5)
    p = jax.nn.softmax(s, axis=-1).astype(xn.dtype)
    o = jnp.einsum('bhtm,bmhd->bthd', p, v.astype(xn.dtype)).reshape(B, T, D_MODEL)
    return o @ w_o


def run_group(x, mem_k, mem_v, rg_h, rg_conv, gla_S,
              norm_mix_g, norm_xa_g, norm_mlp_g, final_norm_g,
              rg_w_in, rg_conv_w, rg_conv_b, rg_w_a, rg_b_a, rg_w_x, rg_b_x, rg_lambda, rg_w_out,
              gla_w_in, gla_w_a2, gla_b_a, gla_norm_g, gla_w_out,
              xa_wq, xa_wo, mlp_w1, mlp_w2):
    hs, convs, Ss = [], [], []
    for layer in range(DEPTH):
        j = layer // N_MIXERS
        xn = rmsnorm(x, norm_mix_g[layer])
        if layer % N_MIXERS == 0:
            out, cb, hl = rg_block(xn, rg_conv[j], rg_h[j], rg_w_in[j], rg_conv_w[j], rg_conv_b[j],
                                   rg_w_a[j], rg_b_a[j], rg_w_x[j], rg_b_x[j], rg_lambda[j], rg_w_out[j])
            convs.append(cb)
            hs.append(hl)
        else:
            out, S = gla_block(xn, gla_S[j], gla_w_in[j], gla_w_a2[j], gla_b_a[j], gla_norm_g[j], gla_w_out[j])
            Ss.append(S)
        x = x + out
        x = x + cross_attn(rmsnorm(x, norm_xa_g[layer]), mem_k[layer], mem_v[layer], xa_wq[layer], xa_wo[layer])
        hid = rmsnorm(x, norm_mlp_g[layer]) @ mlp_w1[layer]
        x = x + jnp.square(jax.nn.relu(hid)) @ mlp_w2[layer]
    y = rmsnorm(x, final_norm_g)
    return y, jnp.stack(hs), jnp.stack(convs), jnp.stack(Ss)


def setup_inputs(seed: int = 0) -> dict:
    key = jax.random.key(seed)
    ks = iter(jax.random.split(key, 48))

    def nrm(shape, s):
        return jax.random.normal(next(ks), shape, jnp.float32) * s

    d = D_MODEL
    u = jax.random.uniform(next(ks), (N_A_LAYERS, D_RNN), jnp.float32, minval=0.9, maxval=0.999)
    return {
        "x_prompt": nrm((BATCH, SEQ, d), 1.0),
        "x_sample": nrm((DEC_BATCH, DEC_SEQ, d), 1.0),
        "mem_prompt": nrm((BATCH, N_MEM, d), 1.0),
        "state_rglru_h": nrm((N_A_LAYERS, DEC_BATCH, D_RNN), 0.5),
        "state_rglru_conv": nrm((N_A_LAYERS, DEC_BATCH, CONV_W - 1, D_RNN), 1.0),
        "state_gla_S": nrm((N_B_LAYERS, DEC_BATCH, GLA_HEADS, GLA_HK, GLA_HV), 1.0),
        "cache_mem_k": nrm((DEPTH, DEC_BATCH, N_MEM, XA_HEADS, XA_HD), 1.0),
        "cache_mem_v": nrm((DEPTH, DEC_BATCH, N_MEM, XA_HEADS, XA_HD), 1.0),
        "norm_mix_g": 1.0 + nrm((DEPTH, d), 0.01),
        "norm_xa_g": 1.0 + nrm((DEPTH, d), 0.01),
        "norm_mem_g": 1.0 + nrm((DEPTH, d), 0.01),
        "norm_mlp_g": 1.0 + nrm((DEPTH, d), 0.01),
        "final_norm_g": 1.0 + nrm((d,), 0.01),
        "rg_w_in": nrm((N_A_LAYERS, d, 2 * D_RNN), d ** -0.5),
        "rg_conv_w": nrm((N_A_LAYERS, CONV_W, D_RNN), CONV_W ** -0.5),
        "rg_conv_b": nrm((N_A_LAYERS, D_RNN), 0.01),
        "rg_w_a": nrm((N_A_LAYERS, RG_BLOCKS, RG_BW, RG_BW), RG_BW ** -0.5),
        "rg_b_a": nrm((N_A_LAYERS, D_RNN), 0.01),
        "rg_w_x": nrm((N_A_LAYERS, RG_BLOCKS, RG_BW, RG_BW), RG_BW ** -0.5),
        "rg_b_x": nrm((N_A_LAYERS, D_RNN), 0.01),
        "rg_lambda": jnp.log(u) - jnp.log1p(-u),
        "rg_w_out": nrm((N_A_LAYERS, D_RNN, d), D_RNN ** -0.5),
        "gla_w_in": nrm((N_B_LAYERS, d, GLA_IN), d ** -0.5),
        "gla_w_a2": nrm((N_B_LAYERS, GLA_RANK, GLA_DK), GLA_RANK ** -0.5),
        "gla_b_a": nrm((N_B_LAYERS, GLA_DK), 0.01),
        "gla_norm_g": 1.0 + nrm((N_B_LAYERS, GLA_HV), 0.01),
        "gla_w_out": nrm((N_B_LAYERS, GLA_DV, d), GLA_DV ** -0.5),
        "xa_wq": nrm((DEPTH, d, d), d ** -0.5),
        "xa_wk": nrm((DEPTH, d, d), d ** -0.5),
        "xa_wv": nrm((DEPTH, d, d), d ** -0.5),
        "xa_wo": nrm((DEPTH, d, d), d ** -0.5),
        "mlp_w1": nrm((DEPTH, d, D_FF), d ** -0.5),
        "mlp_w2": nrm((DEPTH, D_FF, d), D_FF ** -0.5),
    }


def reference(x_prompt, x_sample, mem_prompt, state_rglru_h, state_rglru_conv, state_gla_S,
              cache_mem_k, cache_mem_v,
              norm_mix_g, norm_xa_g, norm_mem_g, norm_mlp_g, final_norm_g,
              rg_w_in, rg_conv_w, rg_conv_b, rg_w_a, rg_b_a, rg_w_x, rg_b_x, rg_lambda, rg_w_out,
              gla_w_in, gla_w_a2, gla_b_a, gla_norm_g, gla_w_out,
              xa_wq, xa_wk, xa_wv, xa_wo, mlp_w1, mlp_w2):
    weights = (norm_mix_g, norm_xa_g, norm_mlp_g, final_norm_g,
               rg_w_in, rg_conv_w, rg_conv_b, rg_w_a, rg_b_a, rg_w_x, rg_b_x, rg_lambda, rg_w_out,
               gla_w_in, gla_w_a2, gla_b_a, gla_norm_g, gla_w_out,
               xa_wq, xa_wo, mlp_w1, mlp_w2)

    kv = [mem_kv(mem_prompt, norm_mem_g[l], xa_wk[l], xa_wv[l]) for l in range(DEPTH)]
    mem_k_prompt = jnp.stack([kv_l[0] for kv_l in kv])
    mem_v_prompt = jnp.stack([kv_l[1] for kv_l in kv])
    dt = x_prompt.dtype
    h0_p = jnp.zeros((N_A_LAYERS, BATCH, D_RNN), dt)
    conv0_p = jnp.zeros((N_A_LAYERS, BATCH, CONV_W - 1, D_RNN), dt)
    S0_p = jnp.zeros((N_B_LAYERS, BATCH, GLA_HEADS, GLA_HK, GLA_HV), dt)
    y_prompt, rglru_h_prompt, rglru_conv_prompt, gla_S_prompt = run_group(
        x_prompt, mem_k_prompt, mem_v_prompt, h0_p, conv0_p, S0_p, *weights)

    y_sample, rglru_h_sample, rglru_conv_sample, gla_S_sample = run_group(
        x_sample, cache_mem_k, cache_mem_v, state_rglru_h, state_rglru_conv, state_gla_S, *weights)

    return (y_prompt, y_sample, mem_k_prompt, mem_v_prompt, rglru_h_prompt, rglru_conv_prompt,
            gla_S_prompt, rglru_h_sample, rglru_conv_sample, gla_S_sample)
```

```python
import functools

import jax
import jax.numpy as jnp
from jax import lax
from jax.experimental import pallas as pl
from jax.experimental.pallas import tpu as pltpu

F32 = jnp.float32
BF16 = jnp.bfloat16

D_MODEL = 1024
DEPTH = 4
D_RNN = D_MODEL
RG_BLOCKS = 16
RG_BW = D_RNN // RG_BLOCKS
CONV_W = 4
RG_C = 8.0
GLA_HEADS = 4
GLA_DK = D_MODEL // 2
GLA_DV = D_MODEL
GLA_HK = GLA_DK // GLA_HEADS
GLA_HV = GLA_DV // GLA_HEADS
GLA_RANK = 16
GLA_TAU = 16.0
GLA_CHUNK = 64
N_MEM = 256
XA_HEADS = 4
XA_HD = D_MODEL // XA_HEADS
D_FF = 4 * D_MODEL
EPS = 1e-6

SUBLANES = 8
LANES = 128
MXU_DIM = 256
VMEM_LIMIT_BYTES = 56 * 1024 * 1024


def _params(n_axes):
    return pltpu.CompilerParams(
        dimension_semantics=("arbitrary",) * n_axes,
        vmem_limit_bytes=VMEM_LIMIT_BYTES,
    )


def _resident(shape):
    nd = len(shape)
    return pl.BlockSpec(shape, lambda *_: (0,) * nd, pipeline_mode=pl.Buffered(1))


def _rmsnorm(x, g):
    ms = jnp.mean(x * x, axis=-1, keepdims=True)
    return x * lax.rsqrt(ms + EPS) * g


def _mm(a, w):
    return jnp.dot(a.astype(BF16), w, preferred_element_type=F32)


def _softplus(z):
    return jnp.maximum(z, 0.0) + jnp.log1p(jnp.exp(-jnp.abs(z)))


def _shift_rows(x, s, axis):
    return pltpu.roll(x, s, axis)


def _memkv_kernel(mem_ref, g_ref, wk_ref, wv_ref, k_ref, v_ref, kb_ref, vb_ref):
    mn = _rmsnorm(mem_ref[...], g_ref[0]).astype(BF16)
    k = jnp.dot(mn, wk_ref[0], preferred_element_type=F32)
    v = jnp.dot(mn, wv_ref[0], preferred_element_type=F32)
    k_ref[0] = k
    v_ref[0] = v
    kb_ref[0] = k.astype(BF16)
    vb_ref[0] = v.astype(BF16)


def _mem_kv(mem2d, norm_g, wk, wv, tm=512):
    n = mem2d.shape[0]
    row = pl.BlockSpec((tm, D_MODEL), lambda l, i: (i, 0))
    per_layer = lambda shape: pl.BlockSpec((1,) + shape, lambda l, i: (l, 0, 0))
    out = pl.BlockSpec((1, tm, D_MODEL), lambda l, i: (l, i, 0))
    return pl.pallas_call(
        _memkv_kernel,
        grid=(DEPTH, n // tm),
        in_specs=[row, per_layer((1, D_MODEL)), per_layer((D_MODEL, D_MODEL)),
                  per_layer((D_MODEL, D_MODEL))],
        out_specs=[out, out, out, out],
        out_shape=[jax.ShapeDtypeStruct((DEPTH, n, D_MODEL), F32)] * 2
        + [jax.ShapeDtypeStruct((DEPTH, n, D_MODEL), BF16)] * 2,
        compiler_params=_params(2),
        name="mem_kv",
    )(mem2d, norm_g.reshape(DEPTH, 1, D_MODEL), wk, wv)


def _mlp_kernel(x_ref, g_ref, w1_ref, w2_ref, gf_ref, o_ref, *, final, f_chunk):
    x = x_ref[...]
    xn = _rmsnorm(x, g_ref[...]).astype(BF16)
    acc = x
    for c in range(D_FF // f_chunk):
        hid = jnp.dot(xn, w1_ref[:, c * f_chunk:(c + 1) * f_chunk], preferred_element_type=F32)
        act = jnp.square(jnp.maximum(hid, 0.0)).astype(BF16)
        acc = acc + jnp.dot(act, w2_ref[c * f_chunk:(c + 1) * f_chunk, :], preferred_element_type=F32)
    if final:
        acc = _rmsnorm(acc, gf_ref[...])
    o_ref[...] = acc


def _mlp(x2d, g, w1, w2, gf, final, tm=512, f_chunk=1024):
    n = x2d.shape[0]
    row = pl.BlockSpec((tm, D_MODEL), lambda i: (i, 0))
    return pl.pallas_call(
        functools.partial(_mlp_kernel, final=final, f_chunk=f_chunk),
        grid=(n // tm,),
        in_specs=[row, _resident((1, D_MODEL)), _resident((D_MODEL, D_FF)),
                  _resident((D_FF, D_MODEL)), _resident((1, D_MODEL))],
        out_specs=row,
        out_shape=jax.ShapeDtypeStruct((n, D_MODEL), F32),
        compiler_params=_params(1),
        name="mlp",
    )(x2d, g.reshape(1, D_MODEL), w1, w2, gf.reshape(1, D_MODEL))


def _xattn_kernel(x_ref, g_ref, wq_ref, wo_ref, k_ref, v_ref, o_ref, att_ref, *, bb, tt):
    n = bb * tt
    x = x_ref[...].reshape(n, D_MODEL)
    xn = _rmsnorm(x, g_ref[...])
    q = _mm(xn, wq_ref[...])
    scale = XA_HD ** -0.5
    for b in range(bb):
        for h in range(XA_HEADS):
            cols = slice(h * XA_HD, (h + 1) * XA_HD)
            qh = q[b * tt:(b + 1) * tt, cols].astype(BF16)
            kh = k_ref[b, :, cols].astype(BF16)
            vh = v_ref[b, :, cols].astype(BF16)
            s = lax.dot_general(qh, kh, (((1,), (1,)), ((), ())),
                                preferred_element_type=F32) * scale
            e = jnp.exp(s - jnp.max(s, axis=-1, keepdims=True))
            p = e / jnp.sum(e, axis=-1, keepdims=True)
            att_ref[b * tt:(b + 1) * tt, cols] = jnp.dot(
                p.astype(BF16), vh, preferred_element_type=F32)
    out = x + _mm(att_ref[...], wo_ref[...])
    o_ref[...] = out.reshape(bb, tt, D_MODEL)


def _xattn(x3d, g, wq, wo, k, v, bb, tt):
    nb, t, _ = x3d.shape
    tile = pl.BlockSpec((bb, tt, D_MODEL), lambda b, i: (b, i, 0))
    kv = pl.BlockSpec((bb, N_MEM, D_MODEL), lambda b, i: (b, 0, 0))
    return pl.pallas_call(
        functools.partial(_xattn_kernel, bb=bb, tt=tt),
        grid=(nb // bb, t // tt),
        in_specs=[tile, _resident((1, D_MODEL)), _resident((D_MODEL, D_MODEL)),
                  _resident((D_MODEL, D_MODEL)), kv, kv],
        out_specs=tile,
        out_shape=jax.ShapeDtypeStruct(x3d.shape, F32),
        scratch_shapes=[pltpu.VMEM((bb * tt, D_MODEL), F32)],
        compiler_params=_params(2),
        name="xattn",
    )(x3d, g.reshape(1, D_MODEL), wq, wo, k, v)


def _rg_kernel(x_ref, buf_ref, h0_ref, g_ref, win_ref, cw_ref, cb_ref, wg_ref, ba_ref, bx_ref,
               lam_ref, wout_ref, o_ref, nbuf_ref, hl_ref, u_ref, hc_ref, *, bb, tt):
    n = bb * tt
    pad = SUBLANES
    i = pl.program_id(1)

    @pl.when(i == 0)
    def _():
        u_ref[:, 0:pad, :] = jnp.zeros((bb, pad, D_RNN), F32)
        u_ref[:, pad - (CONV_W - 1):pad, :] = buf_ref[...]
        hc_ref[...] = h0_ref[...]

    x = x_ref[...].reshape(n, D_MODEL)
    xn = _rmsnorm(x, g_ref[...])
    yx = _mm(xn, win_ref[...])
    gate = jax.nn.gelu(yx[:, :D_RNN])
    u_ref[:, pad:, :] = yx[:, D_RNN:].reshape(bb, tt, D_RNN)

    xc = cb_ref[...].reshape(1, 1, D_RNN) + cw_ref[CONV_W - 1:CONV_W, :].reshape(1, 1, D_RNN) * u_ref[:, pad:, :]
    for j in range(1, CONV_W):
        w_j = cw_ref[CONV_W - 1 - j:CONV_W - j, :].reshape(1, 1, D_RNN)
        xc = xc + w_j * u_ref[:, pad - j:pad - j + tt, :]
    nbuf_ref[...] = u_ref[:, pad + tt - (CONV_W - 1):pad + tt, :]
    u_ref[:, 0:pad, :] = u_ref[:, tt:tt + pad, :]

    xc2 = xc.reshape(n, D_RNN)
    gates = [_mm(xc2[:, c * MXU_DIM:(c + 1) * MXU_DIM], wg_ref[c]) for c in range(D_RNN // MXU_DIM)]
    r = jax.nn.sigmoid(jnp.concatenate([gc[:, :MXU_DIM] for gc in gates], axis=1) + ba_ref[...])
    ig = jax.nn.sigmoid(jnp.concatenate([gc[:, MXU_DIM:] for gc in gates], axis=1) + bx_ref[...])
    log_a = (-RG_C) * r * _softplus(-lam_ref[...])
    a = jnp.exp(log_a)
    mult = jnp.sqrt(-jnp.tanh(log_a) * (a * a + 1.0))
    bt = mult * (ig * xc2)

    a3 = a.reshape(bb, tt, D_RNN)
    b3 = bt.reshape(bb, tt, D_RNN)
    t_idx = lax.broadcasted_iota(jnp.int32, (bb, tt, D_RNN), 1)
    s = 1
    while s < tt:
        keep = t_idx >= s
        a_s = jnp.where(keep, _shift_rows(a3, s, 1), 1.0)
        b_s = jnp.where(keep, _shift_rows(b3, s, 1), 0.0)
        b3 = a3 * b_s + b3
        a3 = a3 * a_s
        s *= 2
    h3 = a3 * hc_ref[...] + b3
    hc_ref[...] = h3[:, tt - 1:tt, :]
    hl_ref[...] = h3[:, tt - 1:tt, :]

    out = x + _mm(gate * h3.reshape(n, D_RNN), wout_ref[...])
    o_ref[...] = out.reshape(bb, tt, D_MODEL)


def _rg_block(x3d, conv_buf, h0, g, w_in, conv_w, conv_b, w_gates, b_a, b_x, lam, w_out, bb, tt):
    nb, t, _ = x3d.shape
    tile = pl.BlockSpec((bb, tt, D_MODEL), lambda b, i: (b, i, 0))
    buf = pl.BlockSpec((bb, CONV_W - 1, D_RNN), lambda b, i: (b, 0, 0))
    hvec = pl.BlockSpec((bb, 1, D_RNN), lambda b, i: (b, 0, 0))
    vec = _resident((1, D_RNN))
    return pl.pallas_call(
        functools.partial(_rg_kernel, bb=bb, tt=tt),
        grid=(nb // bb, t // tt),
        in_specs=[tile, buf, hvec, vec, _resident((D_MODEL, 2 * D_RNN)), _resident((CONV_W, D_RNN)),
                  vec, _resident((D_RNN // MXU_DIM, MXU_DIM, 2 * MXU_DIM)), vec, vec, vec,
                  _resident((D_RNN, D_MODEL))],
        out_specs=[tile, buf, hvec],
        out_shape=[jax.ShapeDtypeStruct(x3d.shape, F32),
                   jax.ShapeDtypeStruct((nb, CONV_W - 1, D_RNN), F32),
                   jax.ShapeDtypeStruct((nb, 1, D_RNN), F32)],
        scratch_shapes=[pltpu.VMEM((bb, tt + SUBLANES, D_RNN), F32),
                        pltpu.VMEM((bb, 1, D_RNN), F32)],
        compiler_params=_params(2),
        name="rg_block",
    )(x3d, conv_buf, h0.reshape(nb, 1, D_RNN), g.reshape(1, D_MODEL), w_in, conv_w,
      conv_b.reshape(1, D_RNN), w_gates, b_a.reshape(1, D_RNN), b_x.reshape(1, D_RNN),
      lam.reshape(1, D_RNN), w_out)


def _block_diag_tiles(w):
    per = MXU_DIM // RG_BW
    w4 = w.reshape(RG_BLOCKS // per, per, RG_BW, RG_BW)
    eye = jnp.eye(per, dtype=w.dtype)
    t = w4[:, :, :, None, :] * eye[None, :, None, :, None]
    return t.reshape(RG_BLOCKS // per, MXU_DIM, MXU_DIM)


_GLA_MAIN = 2 * GLA_DK + 2 * GLA_DV


def _gla_kernel(x_ref, s0_ref, g_ref, wmain_ref, wlo_ref, wa2_ref, ba_ref, ng_ref, wout_ref,
                o_ref, s_ref, q_ref, k_ref, v_ref, la_ref, oc_ref, *, bb, tt, chunk):
    n = bb * tt
    n_seg = n // chunk
    assert bb == 1 or tt == chunk
    shared_state = bb == 1
    i = pl.program_id(1)

    @pl.when(i == 0)
    def _():
        s_ref[...] = s0_ref[...]

    x = x_ref[...].reshape(n, D_MODEL)
    xn = _rmsnorm(x, g_ref[...]).astype(BF16)
    proj = jnp.dot(xn, wmain_ref[...], preferred_element_type=F32)
    q_ref[...] = proj[:, :GLA_DK] * (GLA_HK ** -0.5)
    k_ref[...] = proj[:, GLA_DK:2 * GLA_DK]
    v_ref[...] = proj[:, 2 * GLA_DK:2 * GLA_DK + GLA_DV]
    gsilu = jax.nn.silu(proj[:, 2 * GLA_DK + GLA_DV:])
    a_lo = jnp.dot(xn, wlo_ref[...], preferred_element_type=F32)
    z = _mm(a_lo, wa2_ref[...]) + ba_ref[...]
    la_ref[...] = -_softplus(-z) / GLA_TAU

    t_idx = lax.broadcasted_iota(jnp.int32, (chunk, GLA_HK), 0)
    row = lax.broadcasted_iota(jnp.int32, (chunk, chunk), 0)
    col = lax.broadcasted_iota(jnp.int32, (chunk, chunk), 1)
    causal = row >= col

    def seg_body(j, carry):
        r0 = pl.multiple_of(j * chunk, chunk)
        rows = pl.ds(r0, chunk)
        sb = 0 if shared_state else j
        for h in range(GLA_HEADS):
            kc = slice(h * GLA_HK, (h + 1) * GLA_HK)
            vc = slice(h * GLA_HV, (h + 1) * GLA_HV)
            bcum = la_ref[rows, kc]
            s = 1
            while s < chunk:
                bcum = bcum + jnp.where(t_idx >= s, _shift_rows(bcum, s, 0), 0.0)
                s *= 2
            gl = bcum[chunk - 1:chunk, :]
            qh = q_ref[rows, kc]
            kh = k_ref[rows, kc]
            vh = v_ref[rows, vc].astype(BF16)
            q_in = (qh * jnp.exp(bcum)).astype(BF16)
            k_in = (kh * jnp.exp(-bcum)).astype(BF16)
            k_end = (kh * jnp.exp(gl - bcum)).astype(BF16)
            att = lax.dot_general(q_in, k_in, (((1,), (1,)), ((), ())), preferred_element_type=F32)
            att = jnp.where(causal, att, 0.0).astype(BF16)
            s_old = s_ref[sb, h]
            o = jnp.dot(att, vh, preferred_element_type=F32) + jnp.dot(
                q_in, s_old.astype(BF16), preferred_element_type=F32)
            oc_ref[rows, vc] = o
            eg = jnp.exp(jnp.broadcast_to(gl, (GLA_HK, GLA_HK))).T
            eg = jnp.concatenate([eg] * (GLA_HV // GLA_HK), axis=1)
            s_ref[sb, h] = eg * s_old + lax.dot_general(
                k_end, vh, (((0,), (0,)), ((), ())), preferred_element_type=F32)
        return carry

    lax.fori_loop(0, n_seg, seg_body, 0)

    o = oc_ref[...]
    heads = []
    for h in range(GLA_HEADS):
        heads.append(_rmsnorm(o[:, h * GLA_HV:(h + 1) * GLA_HV], ng_ref[...]))
    on = jnp.concatenate(heads, axis=1)
    out = x + _mm(on * gsilu, wout_ref[...])
    o_ref[...] = out.reshape(bb, tt, D_MODEL)


def _gla_block(x3d, s0, g, w_main, w_lo, w_a2, b_a, norm_g, w_out, bb, tt, chunk):
    nb, t, _ = x3d.shape
    n = bb * tt
    tile = pl.BlockSpec((bb, tt, D_MODEL), lambda b, i: (b, i, 0))
    state = pl.BlockSpec((bb, GLA_HEADS, GLA_HK, GLA_HV), lambda b, i: (b, 0, 0, 0))
    return pl.pallas_call(
        functools.partial(_gla_kernel, bb=bb, tt=tt, chunk=chunk),
        grid=(nb // bb, t // tt),
        in_specs=[tile, state, _resident((1, D_MODEL)), _resident((D_MODEL, _GLA_MAIN)),
                  _resident((D_MODEL, LANES)), _resident((LANES, GLA_DK)), _resident((1, GLA_DK)),
                  _resident((1, GLA_HV)), _resident((GLA_DV, D_MODEL))],
        out_specs=[tile, state],
        out_shape=[jax.ShapeDtypeStruct(x3d.shape, F32), jax.ShapeDtypeStruct(s0.shape, F32)],
        scratch_shapes=[pltpu.VMEM((n, GLA_DK), F32), pltpu.VMEM((n, GLA_DK), F32),
                        pltpu.VMEM((n, GLA_DV), F32), pltpu.VMEM((n, GLA_DK), F32),
                        pltpu.VMEM((n, GLA_DV), F32)],
        compiler_params=_params(2),
        name="gla_block",
    )(x3d, s0, g.reshape(1, D_MODEL), w_main, w_lo, w_a2, b_a.reshape(1, GLA_DK),
      norm_g.reshape(1, GLA_HV), w_out)


def _run_group(x3d, mem_k, mem_v, rg_h, rg_conv, gla_s, w, tiles):
    nb, t, _ = x3d.shape
    hs, convs, states = [], [], []
    x = x3d
    for layer in range(DEPTH):
        j = layer // 2
        if layer % 2 == 0:
            x, cb, hl = _rg_block(x, rg_conv[j], rg_h[j], w["norm_mix_g"][layer], w["rg_w_in"][j],
                                  w["rg_conv_w"][j], w["rg_conv_b"][j], w["rg_w_gates"][j],
                                  w["rg_b_a"][j], w["rg_b_x"][j], w["rg_lambda"][j], w["rg_w_out"][j],
                                  tiles["mix_bb"], tiles["mix_tt"])
            convs.append(cb)
            hs.append(hl.reshape(nb, D_RNN))
        else:
            x, s_new = _gla_block(x, gla_s[j], w["norm_mix_g"][layer], w["gla_w_main"][j],
                                  w["gla_w_lo"][j], w["gla_w_a2"][j], w["gla_b_a"][j],
                                  w["gla_norm_g"][j], w["gla_w_out"][j],
                                  tiles["gla_bb"], tiles["gla_tt"], tiles["gla_chunk"])
            states.append(s_new)
        x = _xattn(x, w["norm_xa_g"][layer], w["xa_wq"][layer], w["xa_wo"][layer],
                   mem_k[layer], mem_v[layer], tiles["xa_bb"], tiles["xa_tt"])
        x = _mlp(x.reshape(nb * t, D_MODEL), w["norm_mlp_g"][layer], w["mlp_w1"][layer],
                 w["mlp_w2"][layer], w["final_norm_g"], final=(layer == DEPTH - 1)).reshape(nb, t, D_MODEL)
    return x, jnp.stack(hs), jnp.stack(convs), jnp.stack(states)


def kernel(x_prompt, x_sample, mem_prompt, state_rglru_h, state_rglru_conv, state_gla_S, cache_mem_k, cache_mem_v, norm_mix_g, norm_xa_g, norm_mem_g, norm_mlp_g, final_norm_g, rg_w_in, rg_conv_w, rg_conv_b, rg_w_a, rg_b_a, rg_w_x, rg_b_x, rg_lambda, rg_w_out, gla_w_in, gla_w_a2, gla_b_a, gla_norm_g, gla_w_out, xa_wq, xa_wk, xa_wv, xa_wo, mlp_w1, mlp_w2):
    batch, seq, _ = x_prompt.shape
    dec_batch, dec_seq, _ = x_sample.shape
    n_a = rg_w_in.shape[0]
    n_b = gla_w_in.shape[0]

    gates = jnp.concatenate(
        [jax.vmap(_block_diag_tiles)(rg_w_a), jax.vmap(_block_diag_tiles)(rg_w_x)], axis=-1)
    lo_pad = LANES - GLA_RANK
    w = dict(
        norm_mix_g=norm_mix_g, norm_xa_g=norm_xa_g, norm_mlp_g=norm_mlp_g, final_norm_g=final_norm_g,
        rg_w_in=rg_w_in.astype(BF16), rg_conv_w=rg_conv_w, rg_conv_b=rg_conv_b,
        rg_w_gates=gates.astype(BF16), rg_b_a=rg_b_a, rg_b_x=rg_b_x, rg_lambda=rg_lambda,
        rg_w_out=rg_w_out.astype(BF16),
        gla_w_main=gla_w_in[:, :, :_GLA_MAIN].astype(BF16),
        gla_w_lo=jnp.pad(gla_w_in[:, :, _GLA_MAIN:], ((0, 0), (0, 0), (0, lo_pad))).astype(BF16),
        gla_w_a2=jnp.pad(gla_w_a2, ((0, 0), (0, lo_pad), (0, 0))).astype(BF16),
        gla_b_a=gla_b_a, gla_norm_g=gla_norm_g, gla_w_out=gla_w_out.astype(BF16),
        xa_wq=xa_wq.astype(BF16), xa_wo=xa_wo.astype(BF16),
        mlp_w1=mlp_w1.astype(BF16), mlp_w2=mlp_w2.astype(BF16),
    )

    mem2d = mem_prompt.reshape(batch * N_MEM, D_MODEL)
    mk, mv, mkb, mvb = _mem_kv(mem2d, norm_mem_g, xa_wk.astype(BF16), xa_wv.astype(BF16))
    mem_k_prompt = mk.reshape(DEPTH, batch, N_MEM, XA_HEADS, XA_HD)
    mem_v_prompt = mv.reshape(DEPTH, batch, N_MEM, XA_HEADS, XA_HD)
    prompt_tiles = dict(mix_bb=1, mix_tt=512, gla_bb=1, gla_tt=512, gla_chunk=GLA_CHUNK,
                        xa_bb=1, xa_tt=512)
    y_p, h_p, conv_p, s_p = _run_group(
        x_prompt,
        mkb.reshape(DEPTH, batch, N_MEM, D_MODEL), mvb.reshape(DEPTH, batch, N_MEM, D_MODEL),
        jnp.zeros((n_a, batch, D_RNN), F32), jnp.zeros((n_a, batch, CONV_W - 1, D_RNN), F32),
        jnp.zeros((n_b, batch, GLA_HEADS, GLA_HK, GLA_HV), F32), w, prompt_tiles)

    sample_chunk = GLA_CHUNK if dec_seq % GLA_CHUNK == 0 else dec_seq
    sample_tiles = dict(mix_bb=32, mix_tt=dec_seq, gla_bb=8, gla_tt=dec_seq, gla_chunk=sample_chunk,
                        xa_bb=8, xa_tt=dec_seq)
    y_s, h_s, conv_s, s_s = _run_group(
        x_sample,
        cache_mem_k.reshape(DEPTH, dec_batch, N_MEM, D_MODEL),
        cache_mem_v.reshape(DEPTH, dec_batch, N_MEM, D_MODEL),
        state_rglru_h, state_rglru_conv, state_gla_S, w, sample_tiles)

    return (y_p, y_s, mem_k_prompt, mem_v_prompt, h_p, conv_p, s_p, h_s, conv_s, s_s)
```

```python
import functools

import jax
import jax.numpy as jnp
from jax import lax
from jax.experimental import pallas as pl
from jax.experimental.pallas import tpu as pltpu

F32 = jnp.float32
BF16 = jnp.bfloat16

D_MODEL = 1024
DEPTH = 4
D_RNN = D_MODEL
RG_BLOCKS = 16
RG_BW = D_RNN // RG_BLOCKS
CONV_W = 4
RG_C = 8.0
GLA_HEADS = 4
GLA_DK = D_MODEL // 2
GLA_DV = D_MODEL
GLA_HK = GLA_DK // GLA_HEADS
GLA_HV = GLA_DV // GLA_HEADS
GLA_RANK = 16
GLA_TAU = 16.0
GLA_CHUNK = 64
N_MEM = 256
XA_HEADS = 4
XA_HD = D_MODEL // XA_HEADS
D_FF = 4 * D_MODEL
EPS = 1e-6

SUBLANES = 8
LANES = 128
MXU_DIM = 256
VMEM_LIMIT_BYTES = 56 * 1024 * 1024
MASKED_SCORE = -1e30


def _params(n_axes):
    return pltpu.CompilerParams(
        dimension_semantics=("arbitrary",) * n_axes,
        vmem_limit_bytes=VMEM_LIMIT_BYTES,
    )


def _layer_spec(shape, layer):
    zeros = (0,) * len(shape)
    return pl.BlockSpec((1,) + tuple(shape), lambda *_: (layer,) + zeros,
                        pipeline_mode=pl.Buffered(1))


def _rmsnorm(x, g):
    ms = jnp.mean(x * x, axis=-1, keepdims=True)
    return x * lax.rsqrt(ms + EPS) * g


def _mm(a, w):
    return jnp.dot(a.astype(BF16), w, preferred_element_type=F32)


def _softplus(z):
    return jnp.maximum(z, 0.0) + jnp.log1p(jnp.exp(-jnp.abs(z)))


def _memkv_kernel(mem_ref, g_ref, wk_ref, wv_ref, k_ref, v_ref, kb_ref, vb_ref, *, nbm):
    mn = _rmsnorm(mem_ref[...], g_ref[0]).astype(BF16)
    k = jnp.dot(mn, wk_ref[0], preferred_element_type=F32)
    v = jnp.dot(mn, wv_ref[0], preferred_element_type=F32)
    k_ref[0] = k.reshape(nbm, N_MEM, XA_HEADS, XA_HD)
    v_ref[0] = v.reshape(nbm, N_MEM, XA_HEADS, XA_HD)
    kb_ref[0] = k.astype(BF16)
    vb_ref[0] = v.astype(BF16)


def _mem_kv(mem2d, norm_g, wk, wv, nbm=2):
    n = mem2d.shape[0]
    nb = n // N_MEM
    tm = nbm * N_MEM
    row = pl.BlockSpec((tm, D_MODEL), lambda l, i: (i, 0))
    per_layer = lambda shape: pl.BlockSpec((1,) + shape, lambda l, i: (l, 0, 0))
    out5 = pl.BlockSpec((1, nbm, N_MEM, XA_HEADS, XA_HD), lambda l, i: (l, i, 0, 0, 0))
    out3 = pl.BlockSpec((1, tm, D_MODEL), lambda l, i: (l, i, 0))
    return pl.pallas_call(
        functools.partial(_memkv_kernel, nbm=nbm),
        grid=(DEPTH, n // tm),
        in_specs=[row, per_layer((1, D_MODEL)), per_layer((D_MODEL, D_MODEL)),
                  per_layer((D_MODEL, D_MODEL))],
        out_specs=[out5, out5, out3, out3],
        out_shape=[jax.ShapeDtypeStruct((DEPTH, nb, N_MEM, XA_HEADS, XA_HD), F32)] * 2
        + [jax.ShapeDtypeStruct((DEPTH, n, D_MODEL), BF16)] * 2,
        compiler_params=_params(2),
        name="mem_kv",
    )(mem2d, norm_g, wk, wv)


def _mlp_kernel(x_ref, g_ref, w1_ref, w2_ref, gf_ref, o_ref, *, final, f_chunk):
    x = x_ref[...]
    xn = _rmsnorm(x, g_ref[0]).astype(BF16)
    acc = x
    for c in range(D_FF // f_chunk):
        hid = jnp.dot(xn, w1_ref[0, :, c * f_chunk:(c + 1) * f_chunk], preferred_element_type=F32)
        act = jnp.square(jnp.maximum(hid, 0.0)).astype(BF16)
        acc = acc + jnp.dot(act, w2_ref[0, c * f_chunk:(c + 1) * f_chunk, :], preferred_element_type=F32)
    if final:
        acc = _rmsnorm(acc, gf_ref[...])
    o_ref[...] = acc


def _mlp(x2d, w, layer, tm=512, f_chunk=1024):
    n = x2d.shape[0]
    final = layer == DEPTH - 1
    row = pl.BlockSpec((tm, D_MODEL), lambda i: (i, 0))
    return pl.pallas_call(
        functools.partial(_mlp_kernel, final=final, f_chunk=f_chunk),
        grid=(n // tm,),
        in_specs=[row, _layer_spec((1, D_MODEL), layer), _layer_spec((D_MODEL, D_FF), layer),
                  _layer_spec((D_FF, D_MODEL), layer),
                  pl.BlockSpec((1, D_MODEL), lambda i: (0, 0))],
        out_specs=row,
        out_shape=jax.ShapeDtypeStruct((n, D_MODEL), F32),
        compiler_params=_params(1),
        name="mlp",
    )(x2d, w["norm_mlp_g"], w["mlp_w1"], w["mlp_w2"], w["final_norm_g"])


def _xattn_kernel(x_ref, g_ref, wq_ref, wo_ref, k_ref, v_ref, o_ref, att_ref, *, bb, tt, packed_heads):
    n = bb * tt
    x = x_ref[...].reshape(n, D_MODEL)
    xn = _rmsnorm(x, g_ref[0])
    q = _mm(xn, wq_ref[0])
    scale = XA_HD ** -0.5

    def softmax(s):
        e = jnp.exp(s - jnp.max(s, axis=-1, keepdims=True))
        return (e / jnp.sum(e, axis=-1, keepdims=True)).astype(BF16)

    nt_dims = (((1,), (1,)), ((), ()))
    if packed_heads:
        shape = (XA_HEADS * tt, N_MEM * XA_HEADS)
        q_head = lax.broadcasted_iota(jnp.int32, shape, 0) // tt
        kv_head = lax.broadcasted_iota(jnp.int32, shape, 1) % XA_HEADS
        same_head = q_head == kv_head
        for b in range(bb):
            qs = jnp.concatenate(
                [q[b * tt:(b + 1) * tt, h * XA_HD:(h + 1) * XA_HD] for h in range(XA_HEADS)],
                axis=0).astype(BF16)
            kf = k_ref[0, b].reshape(N_MEM * XA_HEADS, XA_HD).astype(BF16)
            vf = v_ref[0, b].reshape(N_MEM * XA_HEADS, XA_HD).astype(BF16)
            s = lax.dot_general(qs, kf, nt_dims, preferred_element_type=F32) * scale
            p = softmax(jnp.where(same_head, s, MASKED_SCORE))
            o = jnp.dot(p, vf, preferred_element_type=F32)
            for h in range(XA_HEADS):
                att_ref[b * tt:(b + 1) * tt, h * XA_HD:(h + 1) * XA_HD] = o[h * tt:(h + 1) * tt]
    else:
        for b in range(bb):
            for h in range(XA_HEADS):
                cols = slice(h * XA_HD, (h + 1) * XA_HD)
                qh = q[b * tt:(b + 1) * tt, cols].astype(BF16)
                s = lax.dot_general(qh, k_ref[0, b, :, cols], nt_dims,
                                    preferred_element_type=F32) * scale
                att_ref[b * tt:(b + 1) * tt, cols] = jnp.dot(
                    softmax(s), v_ref[0, b, :, cols], preferred_element_type=F32)
    out = x + _mm(att_ref[...], wo_ref[0])
    o_ref[...] = out.reshape(bb, tt, D_MODEL)


def _xattn(x3d, w, layer, k, v, bb, tt):
    nb, t, _ = x3d.shape
    packed_heads = k.ndim == 5
    tile = pl.BlockSpec((bb, tt, D_MODEL), lambda b, i: (b, i, 0))
    if packed_heads:
        kv = pl.BlockSpec((1, bb, N_MEM, XA_HEADS, XA_HD), lambda b, i: (layer, b, 0, 0, 0))
    else:
        kv = pl.BlockSpec((1, bb, N_MEM, D_MODEL), lambda b, i: (layer, b, 0, 0))
    return pl.pallas_call(
        functools.partial(_xattn_kernel, bb=bb, tt=tt, packed_heads=packed_heads),
        grid=(nb // bb, t // tt),
        in_specs=[tile, _layer_spec((1, D_MODEL), layer), _layer_spec((D_MODEL, D_MODEL), layer),
                  _layer_spec((D_MODEL, D_MODEL), layer), kv, kv],
        out_specs=tile,
        out_shape=jax.ShapeDtypeStruct(x3d.shape, F32),
        scratch_shapes=[pltpu.VMEM((bb * tt, D_MODEL), F32)],
        compiler_params=_params(2),
        name="xattn",
    )(x3d, w["norm_xa_g"], w["xa_wq"], w["xa_wo"], k, v)


def _rg_kernel(x_ref, buf_ref, h0_ref, g_ref, win_ref, cw_ref, cb_ref, wg_ref, ba_ref, bx_ref,
               lam_ref, wout_ref, o_ref, nbuf_ref, hl_ref, u_ref, hc_ref, *, bb, tt):
    n = bb * tt
    pad = SUBLANES
    i = pl.program_id(1)

    @pl.when(i == 0)
    def _():
        u_ref[:, 0:pad, :] = jnp.zeros((bb, pad, D_RNN), F32)
        u_ref[:, pad - (CONV_W - 1):pad, :] = buf_ref[0]
        hc_ref[...] = h0_ref[0]

    x = x_ref[...].reshape(n, D_MODEL)
    xn = _rmsnorm(x, g_ref[0])
    yx = _mm(xn, win_ref[0])
    gate = jax.nn.gelu(yx[:, :D_RNN])
    u_ref[:, pad:, :] = yx[:, D_RNN:].reshape(bb, tt, D_RNN)

    def tap(j):
        return cw_ref[0, CONV_W - 1 - j:CONV_W - j, :].reshape(1, 1, D_RNN)

    xc = cb_ref[0].reshape(1, 1, D_RNN) + tap(0) * u_ref[:, pad:, :]
    for j in range(1, CONV_W):
        xc = xc + tap(j) * u_ref[:, pad - j:pad - j + tt, :]
    nbuf_ref[0] = u_ref[:, pad + tt - (CONV_W - 1):pad + tt, :]
    u_ref[:, 0:pad, :] = u_ref[:, tt:tt + pad, :]

    xc2 = xc.reshape(n, D_RNN)
    gates = [_mm(xc2[:, c * MXU_DIM:(c + 1) * MXU_DIM], wg_ref[0, c]) for c in range(D_RNN // MXU_DIM)]
    r = jax.nn.sigmoid(jnp.concatenate([gc[:, :MXU_DIM] for gc in gates], axis=1) + ba_ref[0])
    ig = jax.nn.sigmoid(jnp.concatenate([gc[:, MXU_DIM:] for gc in gates], axis=1) + bx_ref[0])
    log_a = (-RG_C) * r * _softplus(-lam_ref[0])
    a = jnp.exp(log_a)
    mult = jnp.sqrt(-jnp.tanh(log_a) * (a * a + 1.0))
    bt = mult * (ig * xc2)

    a3 = a.reshape(bb, tt, D_RNN)
    b3 = bt.reshape(bb, tt, D_RNN)
    t_idx = lax.broadcasted_iota(jnp.int32, (bb, tt, D_RNN), 1)
    s = 1
    while s < tt:
        keep = t_idx >= s
        a_s = jnp.where(keep, pltpu.roll(a3, s, 1), 1.0)
        b_s = jnp.where(keep, pltpu.roll(b3, s, 1), 0.0)
        b3 = a3 * b_s + b3
        a3 = a3 * a_s
        s *= 2
    h3 = a3 * hc_ref[...] + b3
    hc_ref[...] = h3[:, tt - 1:tt, :]
    hl_ref[0] = h3[:, tt - 1:tt, :]

    out = x + _mm(gate * h3.reshape(n, D_RNN), wout_ref[0])
    o_ref[...] = out.reshape(bb, tt, D_MODEL)


def _rg_block(x3d, conv_buf, h0, w, layer, bb, tt):
    nb, t, _ = x3d.shape
    j = layer // 2
    tile = pl.BlockSpec((bb, tt, D_MODEL), lambda b, i: (b, i, 0))
    buf_in = pl.BlockSpec((1, bb, CONV_W - 1, D_RNN), lambda b, i: (j, b, 0, 0))
    h_in = pl.BlockSpec((1, bb, 1, D_RNN), lambda b, i: (j, b, 0, 0))
    buf_out = pl.BlockSpec((1, bb, CONV_W - 1, D_RNN), lambda b, i: (0, b, 0, 0))
    h_out = pl.BlockSpec((1, bb, 1, D_RNN), lambda b, i: (0, b, 0, 0))
    vec = _layer_spec((1, D_RNN), j)
    return pl.pallas_call(
        functools.partial(_rg_kernel, bb=bb, tt=tt),
        grid=(nb // bb, t // tt),
        in_specs=[tile, buf_in, h_in, _layer_spec((1, D_MODEL), layer),
                  _layer_spec((D_MODEL, 2 * D_RNN), j), _layer_spec((CONV_W, D_RNN), j), vec,
                  _layer_spec((D_RNN // MXU_DIM, MXU_DIM, 2 * MXU_DIM), j), vec, vec, vec,
                  _layer_spec((D_RNN, D_MODEL), j)],
        out_specs=[tile, buf_out, h_out],
        out_shape=[jax.ShapeDtypeStruct(x3d.shape, F32),
                   jax.ShapeDtypeStruct((1, nb, CONV_W - 1, D_RNN), F32),
                   jax.ShapeDtypeStruct((1, nb, 1, D_RNN), F32)],
        scratch_shapes=[pltpu.VMEM((bb, tt + SUBLANES, D_RNN), F32),
                        pltpu.VMEM((bb, 1, D_RNN), F32)],
        compiler_params=_params(2),
        name="rg_block",
    )(x3d, conv_buf, h0, w["norm_mix_g"], w["rg_w_in"], w["rg_conv_w"], w["rg_conv_b"],
      w["rg_w_gates"], w["rg_b_a"], w["rg_b_x"], w["rg_lambda"], w["rg_w_out"])


def _block_diag_tiles(w):
    per = MXU_DIM // RG_BW
    w4 = w.reshape(RG_BLOCKS // per, per, RG_BW, RG_BW)
    eye = jnp.eye(per, dtype=w.dtype)
    t = w4[:, :, :, None, :] * eye[None, :, None, :, None]
    return t.reshape(RG_BLOCKS // per, MXU_DIM, MXU_DIM)


_GLA_MAIN = 2 * GLA_DK + 2 * GLA_DV


def _gla_kernel(x_ref, s0_ref, g_ref, wmain_ref, wlo_ref, wa2_ref, ba_ref, ng_ref, wout_ref,
                o_ref, s_ref, q_ref, k_ref, v_ref, la_ref, oc_ref, *, bb, tt, chunk):
    n = bb * tt
    n_seg = n // chunk
    assert bb == 1 or tt == chunk
    shared_state = bb == 1
    i = pl.program_id(1)

    @pl.when(i == 0)
    def _():
        s_ref[...] = s0_ref[...]

    x = x_ref[...].reshape(n, D_MODEL)
    xn = _rmsnorm(x, g_ref[0]).astype(BF16)
    proj = jnp.dot(xn, wmain_ref[0], preferred_element_type=F32)
    q_ref[...] = proj[:, :GLA_DK] * (GLA_HK ** -0.5)
    k_ref[...] = proj[:, GLA_DK:2 * GLA_DK]
    v_ref[...] = proj[:, 2 * GLA_DK:2 * GLA_DK + GLA_DV]
    gsilu = jax.nn.silu(proj[:, 2 * GLA_DK + GLA_DV:])
    a_lo = jnp.dot(xn, wlo_ref[0], preferred_element_type=F32)
    z = _mm(a_lo, wa2_ref[0]) + ba_ref[0]
    la_ref[...] = -_softplus(-z) / GLA_TAU

    t_idx = lax.broadcasted_iota(jnp.int32, (chunk, GLA_HK), 0)
    row = lax.broadcasted_iota(jnp.int32, (chunk, chunk), 0)
    col = lax.broadcasted_iota(jnp.int32, (chunk, chunk), 1)
    causal = row >= col

    def seg_body(j, carry):
        r0 = pl.multiple_of(j * chunk, chunk)
        rows = pl.ds(r0, chunk)
        sb = 0 if shared_state else j
        for h in range(GLA_HEADS):
            kc = slice(h * GLA_HK, (h + 1) * GLA_HK)
            vc = slice(h * GLA_HV, (h + 1) * GLA_HV)
            bcum = la_ref[rows, kc]
            s = 1
            while s < chunk:
                bcum = bcum + jnp.where(t_idx >= s, pltpu.roll(bcum, s, 0), 0.0)
                s *= 2
            gl = bcum[chunk - 1:chunk, :]
            qh = q_ref[rows, kc]
            kh = k_ref[rows, kc]
            vh = v_ref[rows, vc].astype(BF16)
            q_in = (qh * jnp.exp(bcum)).astype(BF16)
            k_in = (kh * jnp.exp(-bcum)).astype(BF16)
            k_end = (kh * jnp.exp(gl - bcum)).astype(BF16)
            att = lax.dot_general(q_in, k_in, (((1,), (1,)), ((), ())), preferred_element_type=F32)
            att = jnp.where(causal, att, 0.0).astype(BF16)
            s_old = s_ref[0, sb, h]
            o = jnp.dot(att, vh, preferred_element_type=F32) + jnp.dot(
                q_in, s_old.astype(BF16), preferred_element_type=F32)
            oc_ref[rows, vc] = o
            eg = jnp.exp(jnp.broadcast_to(gl, (GLA_HK, GLA_HK))).T
            eg = jnp.concatenate([eg] * (GLA_HV // GLA_HK), axis=1)
            s_ref[0, sb, h] = eg * s_old + lax.dot_general(
                k_end, vh, (((0,), (0,)), ((), ())), preferred_element_type=F32)
        return carry

    lax.fori_loop(0, n_seg, seg_body, 0)

    o = oc_ref[...]
    heads = []
    for h in range(GLA_HEADS):
        heads.append(_rmsnorm(o[:, h * GLA_HV:(h + 1) * GLA_HV], ng_ref[0]))
    on = jnp.concatenate(heads, axis=1)
    out = x + _mm(on * gsilu, wout_ref[0])
    o_ref[...] = out.reshape(bb, tt, D_MODEL)


def _gla_block(x3d, s0, w, layer, bb, tt, chunk):
    nb, t, _ = x3d.shape
    n = bb * tt
    j = layer // 2
    tile = pl.BlockSpec((bb, tt, D_MODEL), lambda b, i: (b, i, 0))
    state_in = pl.BlockSpec((1, bb, GLA_HEADS, GLA_HK, GLA_HV), lambda b, i: (j, b, 0, 0, 0))
    state_out = pl.BlockSpec((1, bb, GLA_HEADS, GLA_HK, GLA_HV), lambda b, i: (0, b, 0, 0, 0))
    return pl.pallas_call(
        functools.partial(_gla_kernel, bb=bb, tt=tt, chunk=chunk),
        grid=(nb // bb, t // tt),
        in_specs=[tile, state_in, _layer_spec((1, D_MODEL), layer), _layer_spec((D_MODEL, _GLA_MAIN), j),
                  _layer_spec((D_MODEL, LANES), j), _layer_spec((LANES, GLA_DK), j),
                  _layer_spec((1, GLA_DK), j), _layer_spec((1, GLA_HV), j),
                  _layer_spec((GLA_DV, D_MODEL), j)],
        out_specs=[tile, state_out],
        out_shape=[jax.ShapeDtypeStruct(x3d.shape, F32),
                   jax.ShapeDtypeStruct((1,) + s0.shape[1:], F32)],
        scratch_shapes=[pltpu.VMEM((n, GLA_DK), F32), pltpu.VMEM((n, GLA_DK), F32),
                        pltpu.VMEM((n, GLA_DV), F32), pltpu.VMEM((n, GLA_DK), F32),
                        pltpu.VMEM((n, GLA_DV), F32)],
        compiler_params=_params(2),
        name="gla_block",
    )(x3d, s0, w["norm_mix_g"], w["gla_w_main"], w["gla_w_lo"], w["gla_w_a2"], w["gla_b_a"],
      w["gla_norm_g"], w["gla_w_out"])


def _run_group(x3d, mem_k, mem_v, rg_h, rg_conv, gla_s, w, tiles):
    nb, t, _ = x3d.shape
    hs, convs, states = [], [], []
    x = x3d
    for layer in range(DEPTH):
        if layer % 2 == 0:
            x, cb, hl = _rg_block(x, rg_conv, rg_h, w, layer, tiles["mix_bb"], tiles["mix_tt"])
            convs.append(cb)
            hs.append(hl)
        else:
            x, s_new = _gla_block(x, gla_s, w, layer, tiles["gla_bb"], tiles["gla_tt"],
                                  tiles["gla_chunk"])
            states.append(s_new)
        x = _xattn(x, w, layer, mem_k, mem_v, tiles["xa_bb"], tiles["xa_tt"])
        x = _mlp(x.reshape(nb * t, D_MODEL), w, layer).reshape(nb, t, D_MODEL)
    h_all = jnp.concatenate(hs, axis=0).reshape(len(hs), nb, D_RNN)
    return x, h_all, jnp.concatenate(convs, axis=0), jnp.concatenate(states, axis=0)


def kernel(x_prompt, x_sample, mem_prompt, state_rglru_h, state_rglru_conv, state_gla_S, cache_mem_k, cache_mem_v, norm_mix_g, norm_xa_g, norm_mem_g, norm_mlp_g, final_norm_g, rg_w_in, rg_conv_w, rg_conv_b, rg_w_a, rg_b_a, rg_w_x, rg_b_x, rg_lambda, rg_w_out, gla_w_in, gla_w_a2, gla_b_a, gla_norm_g, gla_w_out, xa_wq, xa_wk, xa_wv, xa_wo, mlp_w1, mlp_w2):
    batch, seq, _ = x_prompt.shape
    dec_batch, dec_seq, _ = x_sample.shape
    n_a = rg_w_in.shape[0]
    n_b = gla_w_in.shape[0]

    def rows(p):
        return p.reshape(p.shape[0], 1, p.shape[1])

    gates = jnp.concatenate(
        [jax.vmap(_block_diag_tiles)(rg_w_a), jax.vmap(_block_diag_tiles)(rg_w_x)], axis=-1)
    lo_pad = LANES - GLA_RANK
    w = dict(
        norm_mix_g=rows(norm_mix_g), norm_xa_g=rows(norm_xa_g), norm_mlp_g=rows(norm_mlp_g),
        final_norm_g=final_norm_g.reshape(1, D_MODEL),
        rg_w_in=rg_w_in.astype(BF16), rg_conv_w=rg_conv_w, rg_conv_b=rows(rg_conv_b),
        rg_w_gates=gates.astype(BF16), rg_b_a=rows(rg_b_a), rg_b_x=rows(rg_b_x),
        rg_lambda=rows(rg_lambda), rg_w_out=rg_w_out.astype(BF16),
        gla_w_main=gla_w_in[:, :, :_GLA_MAIN].astype(BF16),
        gla_w_lo=jnp.pad(gla_w_in[:, :, _GLA_MAIN:], ((0, 0), (0, 0), (0, lo_pad))).astype(BF16),
        gla_w_a2=jnp.pad(gla_w_a2, ((0, 0), (0, lo_pad), (0, 0))).astype(BF16),
        gla_b_a=rows(gla_b_a), gla_norm_g=rows(gla_norm_g), gla_w_out=gla_w_out.astype(BF16),
        xa_wq=xa_wq.astype(BF16), xa_wo=xa_wo.astype(BF16),
        mlp_w1=mlp_w1.astype(BF16), mlp_w2=mlp_w2.astype(BF16),
    )

    mem2d = mem_prompt.reshape(batch * N_MEM, D_MODEL)
    mem_k_prompt, mem_v_prompt, mkb, mvb = _mem_kv(
        mem2d, rows(norm_mem_g), xa_wk.astype(BF16), xa_wv.astype(BF16))
    prompt_tiles = dict(mix_bb=1, mix_tt=512, gla_bb=1, gla_tt=512, gla_chunk=GLA_CHUNK,
                        xa_bb=1, xa_tt=512)
    y_p, h_p, conv_p, s_p = _run_group(
        x_prompt,
        mkb.reshape(DEPTH, batch, N_MEM, D_MODEL), mvb.reshape(DEPTH, batch, N_MEM, D_MODEL),
        jnp.zeros((n_a, batch, 1, D_RNN), F32), jnp.zeros((n_a, batch, CONV_W - 1, D_RNN), F32),
        jnp.zeros((n_b, batch, GLA_HEADS, GLA_HK, GLA_HV), F32), w, prompt_tiles)

    sample_chunk = GLA_CHUNK if dec_seq % GLA_CHUNK == 0 else dec_seq
    sample_tiles = dict(mix_bb=32, mix_tt=dec_seq, gla_bb=8, gla_tt=dec_seq, gla_chunk=sample_chunk,
                        xa_bb=8, xa_tt=dec_seq)
    y_s, h_s, conv_s, s_s = _run_group(
        x_sample, cache_mem_k, cache_mem_v,
        state_rglru_h.reshape(n_a, dec_batch, 1, D_RNN), state_rglru_conv, state_gla_S,
        w, sample_tiles)

    return (y_p, y_s, mem_k_prompt, mem_v_prompt, h_p, conv_p, s_p, h_s, conv_s, s_s)
```

```python
import functools

import jax
import jax.numpy as jnp
from jax import lax
from jax.experimental import pallas as pl
from jax.experimental.pallas import tpu as pltpu

F32 = jnp.float32
BF16 = jnp.bfloat16

D_MODEL = 1024
DEPTH = 4
D_RNN = D_MODEL
RG_BLOCKS = 16
RG_BW = D_RNN // RG_BLOCKS
CONV_W = 4
RG_C = 8.0
GLA_HEADS = 4
GLA_DK = D_MODEL // 2
GLA_DV = D_MODEL
GLA_HK = GLA_DK // GLA_HEADS
GLA_HV = GLA_DV // GLA_HEADS
GLA_RANK = 16
GLA_TAU = 16.0
GLA_CHUNK = 64
N_MEM = 256
XA_HEADS = 4
XA_HD = D_MODEL // XA_HEADS
D_FF = 4 * D_MODEL
EPS = 1e-6

SUBLANES = 8
LANES = 128
MXU_DIM = 256
VMEM_LIMIT_BYTES = 56 * 1024 * 1024
MASKED_SCORE = -1e30


def _params(n_axes):
    return pltpu.CompilerParams(
        dimension_semantics=("arbitrary",) * n_axes,
        vmem_limit_bytes=VMEM_LIMIT_BYTES,
    )


def _layer_spec(shape, layer):
    zeros = (0,) * len(shape)
    return pl.BlockSpec((1,) + tuple(shape), lambda *_: (layer,) + zeros,
                        pipeline_mode=pl.Buffered(1))


def _rmsnorm(x, g):
    ms = jnp.mean(x * x, axis=-1, keepdims=True)
    return x * lax.rsqrt(ms + EPS) * g


def _mm(a, w):
    return jnp.dot(a.astype(BF16), w, preferred_element_type=F32)


def _softplus(z):
    return jnp.maximum(z, 0.0) + jnp.log1p(jnp.exp(-jnp.abs(z)))


def _memkv_kernel(mem_ref, g_ref, wk_ref, wv_ref, k_ref, v_ref, kb_ref, vb_ref, *, nbm):
    mn = _rmsnorm(mem_ref[...], g_ref[0]).astype(BF16)
    k = jnp.dot(mn, wk_ref[0], preferred_element_type=F32)
    v = jnp.dot(mn, wv_ref[0], preferred_element_type=F32)
    k_ref[0] = k.reshape(nbm, N_MEM, XA_HEADS, XA_HD)
    v_ref[0] = v.reshape(nbm, N_MEM, XA_HEADS, XA_HD)
    kb_ref[0] = k.astype(BF16)
    vb_ref[0] = v.astype(BF16)


def _mem_kv(mem2d, norm_g, wk, wv, nbm=2):
    n = mem2d.shape[0]
    nb = n // N_MEM
    tm = nbm * N_MEM
    row = pl.BlockSpec((tm, D_MODEL), lambda l, i: (i, 0))
    per_layer = lambda shape: pl.BlockSpec((1,) + shape, lambda l, i: (l, 0, 0))
    out5 = pl.BlockSpec((1, nbm, N_MEM, XA_HEADS, XA_HD), lambda l, i: (l, i, 0, 0, 0))
    out3 = pl.BlockSpec((1, tm, D_MODEL), lambda l, i: (l, i, 0))
    return pl.pallas_call(
        functools.partial(_memkv_kernel, nbm=nbm),
        grid=(DEPTH, n // tm),
        in_specs=[row, per_layer((1, D_MODEL)), per_layer((D_MODEL, D_MODEL)),
                  per_layer((D_MODEL, D_MODEL))],
        out_specs=[out5, out5, out3, out3],
        out_shape=[jax.ShapeDtypeStruct((DEPTH, nb, N_MEM, XA_HEADS, XA_HD), F32)] * 2
        + [jax.ShapeDtypeStruct((DEPTH, n, D_MODEL), BF16)] * 2,
        compiler_params=_params(2),
        name="mem_kv",
    )(mem2d, norm_g, wk, wv)


def _mlp_kernel(x_ref, g_ref, w1_ref, w2_ref, gf_ref, o_ref, *, final, f_chunk):
    x = x_ref[...]
    xn = _rmsnorm(x, g_ref[0]).astype(BF16)
    acc = x
    for c in range(D_FF // f_chunk):
        hid = jnp.dot(xn, w1_ref[0, :, c * f_chunk:(c + 1) * f_chunk], preferred_element_type=F32)
        act = jnp.square(jnp.maximum(hid, 0.0)).astype(BF16)
        acc = acc + jnp.dot(act, w2_ref[0, c * f_chunk:(c + 1) * f_chunk, :], preferred_element_type=F32)
    if final:
        acc = _rmsnorm(acc, gf_ref[...])
    o_ref[...] = acc


def _mlp(x2d, w, layer, tm=512, f_chunk=1024):
    n = x2d.shape[0]
    final = layer == DEPTH - 1
    row = pl.BlockSpec((tm, D_MODEL), lambda i: (i, 0))
    return pl.pallas_call(
        functools.partial(_mlp_kernel, final=final, f_chunk=f_chunk),
        grid=(n // tm,),
        in_specs=[row, _layer_spec((1, D_MODEL), layer), _layer_spec((D_MODEL, D_FF), layer),
                  _layer_spec((D_FF, D_MODEL), layer),
                  pl.BlockSpec((1, D_MODEL), lambda i: (0, 0))],
        out_specs=row,
        out_shape=jax.ShapeDtypeStruct((n, D_MODEL), F32),
        compiler_params=_params(1),
        name="mlp",
    )(x2d, w["norm_mlp_g"], w["mlp_w1"], w["mlp_w2"], w["final_norm_g"])


def _xattn_kernel(x_ref, g_ref, wq_ref, wo_ref, k_ref, v_ref, o_ref, att_ref, *, bb, tt, packed_heads):
    n = bb * tt
    x = x_ref[...].reshape(n, D_MODEL)
    xn = _rmsnorm(x, g_ref[0])
    q = _mm(xn, wq_ref[0])
    scale = XA_HD ** -0.5

    def softmax(s):
        e = jnp.exp(s - jnp.max(s, axis=-1, keepdims=True))
        return (e / jnp.sum(e, axis=-1, keepdims=True)).astype(BF16)

    nt_dims = (((1,), (1,)), ((), ()))
    if packed_heads:
        shape = (XA_HEADS * tt, N_MEM * XA_HEADS)
        q_head = lax.broadcasted_iota(jnp.int32, shape, 0) // tt
        kv_head = lax.broadcasted_iota(jnp.int32, shape, 1) % XA_HEADS
        same_head = q_head == kv_head
        for b in range(bb):
            qs = jnp.concatenate(
                [q[b * tt:(b + 1) * tt, h * XA_HD:(h + 1) * XA_HD] for h in range(XA_HEADS)],
                axis=0).astype(BF16)
            kf = k_ref[0, b].reshape(N_MEM * XA_HEADS, XA_HD).astype(BF16)
            vf = v_ref[0, b].reshape(N_MEM * XA_HEADS, XA_HD).astype(BF16)
            s = lax.dot_general(qs, kf, nt_dims, preferred_element_type=F32) * scale
            p = softmax(jnp.where(same_head, s, MASKED_SCORE))
            o = jnp.dot(p, vf, preferred_element_type=F32)
            for h in range(XA_HEADS):
                att_ref[b * tt:(b + 1) * tt, h * XA_HD:(h + 1) * XA_HD] = o[h * tt:(h + 1) * tt]
    else:
        for b in range(bb):
            for h in range(XA_HEADS):
                cols = slice(h * XA_HD, (h + 1) * XA_HD)
                qh = q[b * tt:(b + 1) * tt, cols].astype(BF16)
                s = lax.dot_general(qh, k_ref[0, b, :, cols], nt_dims,
                                    preferred_element_type=F32) * scale
                att_ref[b * tt:(b + 1) * tt, cols] = jnp.dot(
                    softmax(s), v_ref[0, b, :, cols], preferred_element_type=F32)
    out = x + _mm(att_ref[...], wo_ref[0])
    o_ref[...] = out.reshape(bb, tt, D_MODEL)


def _xattn(x3d, w, layer, k, v, bb, tt):
    nb, t, _ = x3d.shape
    packed_heads = k.ndim == 5
    tile = pl.BlockSpec((bb, tt, D_MODEL), lambda b, i: (b, i, 0))
    if packed_heads:
        kv = pl.BlockSpec((1, bb, N_MEM, XA_HEADS, XA_HD), lambda b, i: (layer, b, 0, 0, 0))
    else:
        kv = pl.BlockSpec((1, bb, N_MEM, D_MODEL), lambda b, i: (layer, b, 0, 0))
    return pl.pallas_call(
        functools.partial(_xattn_kernel, bb=bb, tt=tt, packed_heads=packed_heads),
        grid=(nb // bb, t // tt),
        in_specs=[tile, _layer_spec((1, D_MODEL), layer), _layer_spec((D_MODEL, D_MODEL), layer),
                  _layer_spec((D_MODEL, D_MODEL), layer), kv, kv],
        out_specs=tile,
        out_shape=jax.ShapeDtypeStruct(x3d.shape, F32),
        scratch_shapes=[pltpu.VMEM((bb * tt, D_MODEL), F32)],
        compiler_params=_params(2),
        name="xattn",
    )(x3d, w["norm_xa_g"], w["xa_wq"], w["xa_wo"], k, v)


def _rg_kernel(x_ref, buf_ref, h0_ref, g_ref, win_ref, cw_ref, cb_ref, wg_ref, ba_ref, bx_ref,
               lam_ref, wout_ref, o_ref, nbuf_ref, hl_ref, u_ref, hc_ref, *, bb, tt):
    n = bb * tt
    pad = SUBLANES
    i = pl.program_id(1)

    @pl.when(i == 0)
    def _():
        u_ref[...] = jnp.zeros((bb, pad, D_RNN), F32)
        u_ref[:, pad - (CONV_W - 1):pad, :] = buf_ref[0]
        hc_ref[...] = h0_ref[0]

    x = x_ref[...].reshape(n, D_MODEL)
    xn = _rmsnorm(x, g_ref[0])
    yx = _mm(xn, win_ref[0])
    gate = jax.nn.gelu(yx[:, :D_RNN])
    u3 = yx[:, D_RNN:].reshape(bb, tt, D_RNN)

    ext = jnp.concatenate([u_ref[...], u3], axis=1)

    def tap(j):
        return cw_ref[0, CONV_W - 1 - j:CONV_W - j, :].reshape(1, 1, D_RNN)

    xc = cb_ref[0].reshape(1, 1, D_RNN) + tap(0) * u3
    for j in range(1, CONV_W):
        xc = xc + tap(j) * pltpu.roll(ext, j, 1)[:, pad:, :]
    nbuf_ref[0] = ext[:, pad + tt - (CONV_W - 1):, :]
    u_ref[...] = ext[:, tt:, :]

    xc2 = xc.reshape(n, D_RNN)
    gates = [_mm(xc2[:, c * MXU_DIM:(c + 1) * MXU_DIM], wg_ref[0, c]) for c in range(D_RNN // MXU_DIM)]
    r = jax.nn.sigmoid(jnp.concatenate([gc[:, :MXU_DIM] for gc in gates], axis=1) + ba_ref[0])
    ig = jax.nn.sigmoid(jnp.concatenate([gc[:, MXU_DIM:] for gc in gates], axis=1) + bx_ref[0])
    log_a = (-RG_C) * r * _softplus(-lam_ref[0])
    a = jnp.exp(log_a)
    mult = jnp.sqrt(-jnp.tanh(log_a) * (a * a + 1.0))
    bt = mult * (ig * xc2)

    groups = n // SUBLANES
    ag = a.reshape(groups, SUBLANES, D_RNN)
    bg = bt.reshape(groups, SUBLANES, D_RNN)
    t_idx = lax.broadcasted_iota(jnp.int32, (groups, SUBLANES, D_RNN), 1)
    s = 1
    while s < SUBLANES:
        keep = t_idx >= s
        a_s = jnp.where(keep, pltpu.roll(ag, s, 1), 1.0)
        b_s = jnp.where(keep, pltpu.roll(bg, s, 1), 0.0)
        bg = ag * b_s + bg
        ag = ag * a_s
        s *= 2
    if tt == SUBLANES:
        h = (ag * hc_ref[...] + bg).reshape(n, D_RNN)
        h_last = h.reshape(bb, tt, D_RNN)[:, tt - 1:tt, :]
    else:
        assert bb == 1
        carry = hc_ref[0]
        pieces = []
        for gi in range(groups):
            hg = ag[gi] * carry + bg[gi]
            pieces.append(hg)
            carry = hg[SUBLANES - 1:SUBLANES, :]
        h = jnp.concatenate(pieces, axis=0)
        h_last = carry.reshape(1, 1, D_RNN)
    hc_ref[...] = h_last
    hl_ref[0] = h_last

    out = x + _mm(gate * h, wout_ref[0])
    o_ref[...] = out.reshape(bb, tt, D_MODEL)


def _rg_block(x3d, conv_buf, h0, w, layer, bb, tt):
    nb, t, _ = x3d.shape
    j = layer // 2
    tile = pl.BlockSpec((bb, tt, D_MODEL), lambda b, i: (b, i, 0))
    buf_in = pl.BlockSpec((1, bb, CONV_W - 1, D_RNN), lambda b, i: (j, b, 0, 0))
    h_in = pl.BlockSpec((1, bb, 1, D_RNN), lambda b, i: (j, b, 0, 0))
    buf_out = pl.BlockSpec((1, bb, CONV_W - 1, D_RNN), lambda b, i: (0, b, 0, 0))
    h_out = pl.BlockSpec((1, bb, 1, D_RNN), lambda b, i: (0, b, 0, 0))
    vec = _layer_spec((1, D_RNN), j)
    return pl.pallas_call(
        functools.partial(_rg_kernel, bb=bb, tt=tt),
        grid=(nb // bb, t // tt),
        in_specs=[tile, buf_in, h_in, _layer_spec((1, D_MODEL), layer),
                  _layer_spec((D_MODEL, 2 * D_RNN), j), _layer_spec((CONV_W, D_RNN), j), vec,
                  _layer_spec((D_RNN // MXU_DIM, MXU_DIM, 2 * MXU_DIM), j), vec, vec, vec,
                  _layer_spec((D_RNN, D_MODEL), j)],
        out_specs=[tile, buf_out, h_out],
        out_shape=[jax.ShapeDtypeStruct(x3d.shape, F32),
                   jax.ShapeDtypeStruct((1, nb, CONV_W - 1, D_RNN), F32),
                   jax.ShapeDtypeStruct((1, nb, 1, D_RNN), F32)],
        scratch_shapes=[pltpu.VMEM((bb, SUBLANES, D_RNN), F32),
                        pltpu.VMEM((bb, 1, D_RNN), F32)],
        compiler_params=_params(2),
        name="rg_block",
    )(x3d, conv_buf, h0, w["norm_mix_g"], w["rg_w_in"], w["rg_conv_w"], w["rg_conv_b"],
      w["rg_w_gates"], w["rg_b_a"], w["rg_b_x"], w["rg_lambda"], w["rg_w_out"])


def _block_diag_tiles(w):
    per = MXU_DIM // RG_BW
    w4 = w.reshape(RG_BLOCKS // per, per, RG_BW, RG_BW)
    eye = jnp.eye(per, dtype=w.dtype)
    t = w4[:, :, :, None, :] * eye[None, :, None, :, None]
    return t.reshape(RG_BLOCKS // per, MXU_DIM, MXU_DIM)


_GLA_MAIN = 2 * GLA_DK + 2 * GLA_DV


def _gla_kernel(x_ref, s0_ref, g_ref, wmain_ref, wlo_ref, wa2_ref, ba_ref, ng_ref, wout_ref,
                o_ref, s_ref, oc_ref, *, bb, tt, chunk):
    n = bb * tt
    n_seg = n // chunk
    assert bb == 1 or tt == chunk
    assert chunk & (chunk - 1) == 0
    shared_state = bb == 1
    i = pl.program_id(1)

    @pl.when(i == 0)
    def _():
        s_ref[...] = s0_ref[...]

    x = x_ref[...].reshape(n, D_MODEL)
    xn = _rmsnorm(x, g_ref[0]).astype(BF16)
    proj = jnp.dot(xn, wmain_ref[0], preferred_element_type=F32)
    q3 = (proj[:, :GLA_DK] * (GLA_HK ** -0.5)).reshape(n_seg, chunk, GLA_DK)
    k3 = proj[:, GLA_DK:2 * GLA_DK].reshape(n_seg, chunk, GLA_DK)
    v = proj[:, 2 * GLA_DK:2 * GLA_DK + GLA_DV]
    gsilu = jax.nn.silu(proj[:, 2 * GLA_DK + GLA_DV:])
    a_lo = jnp.dot(xn, wlo_ref[0], preferred_element_type=F32)
    z = _mm(a_lo, wa2_ref[0]) + ba_ref[0]
    la3 = (-_softplus(-z) / GLA_TAU).reshape(n_seg, chunk, GLA_DK)

    t_idx = lax.broadcasted_iota(jnp.int32, (n_seg, chunk, GLA_DK), 1)
    bcum = la3
    s = 1
    while s < chunk:
        bcum = bcum + jnp.where(t_idx >= s, pltpu.roll(bcum, s, 1), 0.0)
        s *= 2
    gl = bcum[:, chunk - 1:chunk, :]
    q_in = (q3 * jnp.exp(bcum)).reshape(n, GLA_DK)
    k_in = (k3 * jnp.exp(-bcum)).reshape(n, GLA_DK)
    k_end = (k3 * jnp.exp(gl - bcum)).reshape(n, GLA_DK)
    if chunk % (2 * SUBLANES) == 0:
        q_in, k_in, k_end, v = (t.astype(BF16) for t in (q_in, k_in, k_end, v))

    rg = min(n, MXU_DIM)
    row = lax.broadcasted_iota(jnp.int32, (rg, rg), 0)
    col = lax.broadcasted_iota(jnp.int32, (rg, rg), 1)
    same_chunk_causal = (col <= row) & (col >= (row & (-chunk)))
    for r in range(n // rg):
        rows = slice(r * rg, (r + 1) * rg)
        for h in range(GLA_HEADS):
            kc = slice(h * GLA_HK, (h + 1) * GLA_HK)
            vc = slice(h * GLA_HV, (h + 1) * GLA_HV)
            att = lax.dot_general(q_in[rows, kc].astype(BF16), k_in[rows, kc].astype(BF16),
                                  (((1,), (1,)), ((), ())), preferred_element_type=F32)
            att = jnp.where(same_chunk_causal, att, 0.0).astype(BF16)
            oc_ref[rows, vc] = jnp.dot(att, v[rows, vc].astype(BF16), preferred_element_type=F32)

    for h in range(GLA_HEADS):
        kc = slice(h * GLA_HK, (h + 1) * GLA_HK)
        vc = slice(h * GLA_HV, (h + 1) * GLA_HV)
        state = None
        for c in range(n_seg):
            rows = slice(c * chunk, (c + 1) * chunk)
            sb = 0 if shared_state else c
            if state is None or not shared_state:
                state = s_ref[0, sb, h]
            oc_ref[rows, vc] = oc_ref[rows, vc] + jnp.dot(
                q_in[rows, kc].astype(BF16), state.astype(BF16), preferred_element_type=F32)
            eg = jnp.exp(jnp.broadcast_to(gl[c][:, kc], (GLA_HK, GLA_HK))).T
            eg = jnp.concatenate([eg] * (GLA_HV // GLA_HK), axis=1)
            state = eg * state + lax.dot_general(
                k_end[rows, kc].astype(BF16), v[rows, vc].astype(BF16),
                (((0,), (0,)), ((), ())), preferred_element_type=F32)
            if not shared_state or c == n_seg - 1:
                s_ref[0, sb, h] = state

    o = oc_ref[...]
    heads = []
    for h in range(GLA_HEADS):
        heads.append(_rmsnorm(o[:, h * GLA_HV:(h + 1) * GLA_HV], ng_ref[0]))
    on = jnp.concatenate(heads, axis=1)
    out = x + _mm(on * gsilu, wout_ref[0])
    o_ref[...] = out.reshape(bb, tt, D_MODEL)


def _gla_block(x3d, s0, w, layer, bb, tt, chunk):
    nb, t, _ = x3d.shape
    n = bb * tt
    j = layer // 2
    tile = pl.BlockSpec((bb, tt, D_MODEL), lambda b, i: (b, i, 0))
    state_in = pl.BlockSpec((1, bb, GLA_HEADS, GLA_HK, GLA_HV), lambda b, i: (j, b, 0, 0, 0))
    state_out = pl.BlockSpec((1, bb, GLA_HEADS, GLA_HK, GLA_HV), lambda b, i: (0, b, 0, 0, 0))
    return pl.pallas_call(
        functools.partial(_gla_kernel, bb=bb, tt=tt, chunk=chunk),
        grid=(nb // bb, t // tt),
        in_specs=[tile, state_in, _layer_spec((1, D_MODEL), layer), _layer_spec((D_MODEL, _GLA_MAIN), j),
                  _layer_spec((D_MODEL, LANES), j), _layer_spec((LANES, GLA_DK), j),
                  _layer_spec((1, GLA_DK), j), _layer_spec((1, GLA_HV), j),
                  _layer_spec((GLA_DV, D_MODEL), j)],
        out_specs=[tile, state_out],
        out_shape=[jax.ShapeDtypeStruct(x3d.shape, F32),
                   jax.ShapeDtypeStruct((1,) + s0.shape[1:], F32)],
        scratch_shapes=[pltpu.VMEM((n, GLA_DV), F32)],
        compiler_params=_params(2),
        name="gla_block",
    )(x3d, s0, w["norm_mix_g"], w["gla_w_main"], w["gla_w_lo"], w["gla_w_a2"], w["gla_b_a"],
      w["gla_norm_g"], w["gla_w_out"])


def _run_group(x3d, mem_k, mem_v, rg_h, rg_conv, gla_s, w, tiles):
    nb, t, _ = x3d.shape
    hs, convs, states = [], [], []
    x = x3d
    for layer in range(DEPTH):
        if layer % 2 == 0:
            x, cb, hl = _rg_block(x, rg_conv, rg_h, w, layer, tiles["mix_bb"], tiles["mix_tt"])
            convs.append(cb)
            hs.append(hl)
        else:
            x, s_new = _gla_block(x, gla_s, w, layer, tiles["gla_bb"], tiles["gla_tt"],
                                  tiles["gla_chunk"])
            states.append(s_new)
        x = _xattn(x, w, layer, mem_k, mem_v, tiles["xa_bb"], tiles["xa_tt"])
        x = _mlp(x.reshape(nb * t, D_MODEL), w, layer).reshape(nb, t, D_MODEL)
    h_all = jnp.concatenate(hs, axis=0).reshape(len(hs), nb, D_RNN)
    return x, h_all, jnp.concatenate(convs, axis=0), jnp.concatenate(states, axis=0)


def kernel(x_prompt, x_sample, mem_prompt, state_rglru_h, state_rglru_conv, state_gla_S, cache_mem_k, cache_mem_v, norm_mix_g, norm_xa_g, norm_mem_g, norm_mlp_g, final_norm_g, rg_w_in, rg_conv_w, rg_conv_b, rg_w_a, rg_b_a, rg_w_x, rg_b_x, rg_lambda, rg_w_out, gla_w_in, gla_w_a2, gla_b_a, gla_norm_g, gla_w_out, xa_wq, xa_wk, xa_wv, xa_wo, mlp_w1, mlp_w2):
    batch, seq, _ = x_prompt.shape
    dec_batch, dec_seq, _ = x_sample.shape
    n_a = rg_w_in.shape[0]
    n_b = gla_w_in.shape[0]

    def rows(p):
        return p.reshape(p.shape[0], 1, p.shape[1])

    gates = jnp.concatenate(
        [jax.vmap(_block_diag_tiles)(rg_w_a), jax.vmap(_block_diag_tiles)(rg_w_x)], axis=-1)
    lo_pad = LANES - GLA_RANK
    w = dict(
        norm_mix_g=rows(norm_mix_g), norm_xa_g=rows(norm_xa_g), norm_mlp_g=rows(norm_mlp_g),
        final_norm_g=final_norm_g.reshape(1, D_MODEL),
        rg_w_in=rg_w_in.astype(BF16), rg_conv_w=rg_conv_w, rg_conv_b=rows(rg_conv_b),
        rg_w_gates=gates.astype(BF16), rg_b_a=rows(rg_b_a), rg_b_x=rows(rg_b_x),
        rg_lambda=rows(rg_lambda), rg_w_out=rg_w_out.astype(BF16),
        gla_w_main=gla_w_in[:, :, :_GLA_MAIN].astype(BF16),
        gla_w_lo=jnp.pad(gla_w_in[:, :, _GLA_MAIN:], ((0, 0), (0, 0), (0, lo_pad))).astype(BF16),
        gla_w_a2=jnp.pad(gla_w_a2, ((0, 0), (0, lo_pad), (0, 0))).astype(BF16),
        gla_b_a=rows(gla_b_a), gla_norm_g=rows(gla_norm_g), gla_w_out=gla_w_out.astype(BF16),
        xa_wq=xa_wq.astype(BF16), xa_wo=xa_wo.astype(BF16),
        mlp_w1=mlp_w1.astype(BF16), mlp_w2=mlp_w2.astype(BF16),
    )

    mem2d = mem_prompt.reshape(batch * N_MEM, D_MODEL)
    mem_k_prompt, mem_v_prompt, mkb, mvb = _mem_kv(
        mem2d, rows(norm_mem_g), xa_wk.astype(BF16), xa_wv.astype(BF16))
    prompt_tiles = dict(mix_bb=1, mix_tt=512, gla_bb=1, gla_tt=512, gla_chunk=GLA_CHUNK,
                        xa_bb=1, xa_tt=512)
    y_p, h_p, conv_p, s_p = _run_group(
        x_prompt,
        mkb.reshape(DEPTH, batch, N_MEM, D_MODEL), mvb.reshape(DEPTH, batch, N_MEM, D_MODEL),
        jnp.zeros((n_a, batch, 1, D_RNN), F32), jnp.zeros((n_a, batch, CONV_W - 1, D_RNN), F32),
        jnp.zeros((n_b, batch, GLA_HEADS, GLA_HK, GLA_HV), F32), w, prompt_tiles)

    sample_chunk = GLA_CHUNK if dec_seq % GLA_CHUNK == 0 else dec_seq
    sample_tiles = dict(mix_bb=32, mix_tt=dec_seq, gla_bb=8, gla_tt=dec_seq, gla_chunk=sample_chunk,
                        xa_bb=8, xa_tt=dec_seq)
    y_s, h_s, conv_s, s_s = _run_group(
        x_sample, cache_mem_k, cache_mem_v,
        state_rglru_h.reshape(n_a, dec_batch, 1, D_RNN), state_rglru_conv, state_gla_S,
        w, sample_tiles)

    return (y_p, y_s, mem_k_prompt, mem_v_prompt, h_p, conv_p, s_p, h_s, conv_s, s_s)
```

```python
import functools

import jax
import jax.numpy as jnp
from jax import lax
from jax.experimental import pallas as pl
from jax.experimental.pallas import tpu as pltpu

F32 = jnp.float32
BF16 = jnp.bfloat16

D_MODEL = 1024
DEPTH = 4
D_RNN = D_MODEL
RG_BLOCKS = 16
RG_BW = D_RNN // RG_BLOCKS
CONV_W = 4
RG_C = 8.0
GLA_HEADS = 4
GLA_DK = D_MODEL // 2
GLA_DV = D_MODEL
GLA_HK = GLA_DK // GLA_HEADS
GLA_HV = GLA_DV // GLA_HEADS
GLA_RANK = 16
GLA_TAU = 16.0
GLA_CHUNK = 64
N_MEM = 256
XA_HEADS = 4
XA_HD = D_MODEL // XA_HEADS
D_FF = 4 * D_MODEL
EPS = 1e-6

SUBLANES = 8
LANES = 128
MXU_DIM = 256
VMEM_LIMIT_BYTES = 56 * 1024 * 1024
MASKED_SCORE = -1e30


def _params(n_axes):
    return pltpu.CompilerParams(
        dimension_semantics=("arbitrary",) * n_axes,
        vmem_limit_bytes=VMEM_LIMIT_BYTES,
    )


def _layer_spec(shape, layer):
    zeros = (0,) * len(shape)
    return pl.BlockSpec((1,) + tuple(shape), lambda *_: (layer,) + zeros,
                        pipeline_mode=pl.Buffered(1))


def _rmsnorm(x, g):
    ms = jnp.mean(x * x, axis=-1, keepdims=True)
    return x * lax.rsqrt(ms + EPS) * g


def _mm(a, w):
    return jnp.dot(a.astype(BF16), w, preferred_element_type=F32)


def _softplus(z):
    return jnp.maximum(z, 0.0) + jnp.log1p(jnp.exp(-jnp.abs(z)))


def _memkv_kernel(mem_ref, g_ref, wk_ref, wv_ref, k_ref, v_ref, kb_ref, vb_ref, *, nbm):
    mn = _rmsnorm(mem_ref[...], g_ref[0]).astype(BF16)
    k = jnp.dot(mn, wk_ref[0], preferred_element_type=F32)
    v = jnp.dot(mn, wv_ref[0], preferred_element_type=F32)
    k_ref[0] = k.reshape(nbm, N_MEM, XA_HEADS, XA_HD)
    v_ref[0] = v.reshape(nbm, N_MEM, XA_HEADS, XA_HD)
    kb_ref[0] = k.astype(BF16)
    vb_ref[0] = v.astype(BF16)


def _mem_kv(mem2d, norm_g, wk, wv, nbm=2):
    n = mem2d.shape[0]
    nb = n // N_MEM
    tm = nbm * N_MEM
    row = pl.BlockSpec((tm, D_MODEL), lambda l, i: (i, 0))
    per_layer = lambda shape: pl.BlockSpec((1,) + shape, lambda l, i: (l, 0, 0))
    out5 = pl.BlockSpec((1, nbm, N_MEM, XA_HEADS, XA_HD), lambda l, i: (l, i, 0, 0, 0))
    out3 = pl.BlockSpec((1, tm, D_MODEL), lambda l, i: (l, i, 0))
    return pl.pallas_call(
        functools.partial(_memkv_kernel, nbm=nbm),
        grid=(DEPTH, n // tm),
        in_specs=[row, per_layer((1, D_MODEL)), per_layer((D_MODEL, D_MODEL)),
                  per_layer((D_MODEL, D_MODEL))],
        out_specs=[out5, out5, out3, out3],
        out_shape=[jax.ShapeDtypeStruct((DEPTH, nb, N_MEM, XA_HEADS, XA_HD), F32)] * 2
        + [jax.ShapeDtypeStruct((DEPTH, n, D_MODEL), BF16)] * 2,
        compiler_params=_params(2),
        name="mem_kv",
    )(mem2d, norm_g, wk, wv)


def _mlp_kernel(x_ref, g_ref, w1_ref, w2_ref, gf_ref, o_ref, *, final, f_chunk):
    x = x_ref[...]
    xn = _rmsnorm(x, g_ref[0]).astype(BF16)
    acc = x
    for c in range(D_FF // f_chunk):
        hid = jnp.dot(xn, w1_ref[0, :, c * f_chunk:(c + 1) * f_chunk], preferred_element_type=F32)
        act = jnp.square(jnp.maximum(hid, 0.0)).astype(BF16)
        acc = acc + jnp.dot(act, w2_ref[0, c * f_chunk:(c + 1) * f_chunk, :], preferred_element_type=F32)
    if final:
        acc = _rmsnorm(acc, gf_ref[...])
    o_ref[...] = acc


def _mlp(x2d, w, layer, tm=512, f_chunk=1024):
    n = x2d.shape[0]
    final = layer == DEPTH - 1
    row = pl.BlockSpec((tm, D_MODEL), lambda i: (i, 0))
    return pl.pallas_call(
        functools.partial(_mlp_kernel, final=final, f_chunk=f_chunk),
        grid=(n // tm,),
        in_specs=[row, _layer_spec((1, D_MODEL), layer), _layer_spec((D_MODEL, D_FF), layer),
                  _layer_spec((D_FF, D_MODEL), layer),
                  pl.BlockSpec((1, D_MODEL), lambda i: (0, 0))],
        out_specs=row,
        out_shape=jax.ShapeDtypeStruct((n, D_MODEL), F32),
        compiler_params=_params(1),
        name="mlp",
    )(x2d, w["norm_mlp_g"], w["mlp_w1"], w["mlp_w2"], w["final_norm_g"])


def _xattn_kernel(x_ref, g_ref, wq_ref, wo_ref, k_ref, v_ref, o_ref, att_ref, *, bb, tt, packed_heads):
    n = bb * tt
    x = x_ref[...].reshape(n, D_MODEL)
    xn = _rmsnorm(x, g_ref[0])
    q = _mm(xn, wq_ref[0])
    scale = XA_HD ** -0.5

    def softmax(s):
        e = jnp.exp(s - jnp.max(s, axis=-1, keepdims=True))
        return (e / jnp.sum(e, axis=-1, keepdims=True)).astype(BF16)

    nt_dims = (((1,), (1,)), ((), ()))
    if packed_heads:
        rows_b = XA_HEADS * tt
        shape = (bb * rows_b, N_MEM * XA_HEADS)
        q_head = (lax.broadcasted_iota(jnp.int32, shape, 0) % rows_b) // tt
        kv_head = lax.broadcasted_iota(jnp.int32, shape, 1) % XA_HEADS
        same_head = q_head == kv_head
        scores = []
        for b in range(bb):
            qs = jnp.concatenate(
                [q[b * tt:(b + 1) * tt, h * XA_HD:(h + 1) * XA_HD] for h in range(XA_HEADS)],
                axis=0).astype(BF16)
            kf = k_ref[0, b].reshape(N_MEM * XA_HEADS, XA_HD).astype(BF16)
            scores.append(lax.dot_general(qs, kf, nt_dims, preferred_element_type=F32))
        s = jnp.concatenate(scores, axis=0) * scale
        p = softmax(jnp.where(same_head, s, MASKED_SCORE))
        for b in range(bb):
            vf = v_ref[0, b].reshape(N_MEM * XA_HEADS, XA_HD).astype(BF16)
            o = jnp.dot(p[b * rows_b:(b + 1) * rows_b], vf, preferred_element_type=F32)
            for h in range(XA_HEADS):
                att_ref[b * tt:(b + 1) * tt, h * XA_HD:(h + 1) * XA_HD] = o[h * tt:(h + 1) * tt]
    else:
        qb = q.astype(BF16)
        for b in range(bb):
            rows = slice(b * tt, (b + 1) * tt)
            head_cols = [slice(h * XA_HD, (h + 1) * XA_HD) for h in range(XA_HEADS)]
            s = jnp.concatenate(
                [lax.dot_general(qb[rows, cols], k_ref[0, b, :, cols], nt_dims,
                                 preferred_element_type=F32) for cols in head_cols], axis=0)
            p = softmax(s * scale)
            for h, cols in enumerate(head_cols):
                att_ref[rows, cols] = jnp.dot(
                    p[h * tt:(h + 1) * tt], v_ref[0, b, :, cols], preferred_element_type=F32)
    out = x + _mm(att_ref[...], wo_ref[0])
    o_ref[...] = out.reshape(bb, tt, D_MODEL)


def _xattn(x3d, w, layer, k, v, bb, tt):
    nb, t, _ = x3d.shape
    packed_heads = k.ndim == 5
    tile = pl.BlockSpec((bb, tt, D_MODEL), lambda b, i: (b, i, 0))
    if packed_heads:
        kv = pl.BlockSpec((1, bb, N_MEM, XA_HEADS, XA_HD), lambda b, i: (layer, b, 0, 0, 0))
    else:
        kv = pl.BlockSpec((1, bb, N_MEM, D_MODEL), lambda b, i: (layer, b, 0, 0))
    return pl.pallas_call(
        functools.partial(_xattn_kernel, bb=bb, tt=tt, packed_heads=packed_heads),
        grid=(nb // bb, t // tt),
        in_specs=[tile, _layer_spec((1, D_MODEL), layer), _layer_spec((D_MODEL, D_MODEL), layer),
                  _layer_spec((D_MODEL, D_MODEL), layer), kv, kv],
        out_specs=tile,
        out_shape=jax.ShapeDtypeStruct(x3d.shape, F32),
        scratch_shapes=[pltpu.VMEM((bb * tt, D_MODEL), F32)],
        compiler_params=_params(2),
        name="xattn",
    )(x3d, w["norm_xa_g"], w["xa_wq"], w["xa_wo"], k, v)


def _rg_kernel(x_ref, buf_ref, h0_ref, g_ref, win_ref, cw_ref, cb_ref, wg_ref, ba_ref, bx_ref,
               lam_ref, wout_ref, o_ref, nbuf_ref, hl_ref, u_ref, hc_ref, *, bb, tt):
    n = bb * tt
    pad = SUBLANES
    i = pl.program_id(1)

    @pl.when(i == 0)
    def _():
        u_ref[...] = jnp.zeros((bb, pad, D_RNN), F32)
        u_ref[:, pad - (CONV_W - 1):pad, :] = buf_ref[0]
        hc_ref[...] = h0_ref[0]

    x = x_ref[...].reshape(n, D_MODEL)
    xn = _rmsnorm(x, g_ref[0])
    yx = _mm(xn, win_ref[0])
    gate = jax.nn.gelu(yx[:, :D_RNN])
    u3 = yx[:, D_RNN:].reshape(bb, tt, D_RNN)

    ext = jnp.concatenate([u_ref[...], u3], axis=1)

    def tap(j):
        return cw_ref[0, CONV_W - 1 - j:CONV_W - j, :].reshape(1, 1, D_RNN)

    xc = cb_ref[0].reshape(1, 1, D_RNN) + tap(0) * u3
    for j in range(1, CONV_W):
        xc = xc + tap(j) * pltpu.roll(ext, j, 1)[:, pad:, :]
    nbuf_ref[0] = ext[:, pad + tt - (CONV_W - 1):, :]
    u_ref[...] = ext[:, tt:, :]

    xc2 = xc.reshape(n, D_RNN)
    gates = [_mm(xc2[:, c * MXU_DIM:(c + 1) * MXU_DIM], wg_ref[0, c]) for c in range(D_RNN // MXU_DIM)]
    r = jax.nn.sigmoid(jnp.concatenate([gc[:, :MXU_DIM] for gc in gates], axis=1) + ba_ref[0])
    ig = jax.nn.sigmoid(jnp.concatenate([gc[:, MXU_DIM:] for gc in gates], axis=1) + bx_ref[0])
    log_a = (-RG_C) * r * _softplus(-lam_ref[0])
    a = jnp.exp(log_a)
    mult = jnp.sqrt(-jnp.tanh(log_a) * (a * a + 1.0))
    bt = mult * (ig * xc2)

    groups = n // SUBLANES
    ag = a.reshape(groups, SUBLANES, D_RNN)
    bg = bt.reshape(groups, SUBLANES, D_RNN)
    t_idx = lax.broadcasted_iota(jnp.int32, (groups, SUBLANES, D_RNN), 1)
    s = 1
    while s < SUBLANES:
        keep = t_idx >= s
        a_s = jnp.where(keep, pltpu.roll(ag, s, 1), 1.0)
        b_s = jnp.where(keep, pltpu.roll(bg, s, 1), 0.0)
        bg = ag * b_s + bg
        ag = ag * a_s
        s *= 2
    if tt == SUBLANES:
        h = (ag * hc_ref[...] + bg).reshape(n, D_RNN)
        h_last = h.reshape(bb, tt, D_RNN)[:, tt - 1:tt, :]
    else:
        assert bb == 1
        carry = hc_ref[0]
        pieces = []
        for gi in range(groups):
            hg = ag[gi] * carry + bg[gi]
            pieces.append(hg)
            carry = hg[SUBLANES - 1:SUBLANES, :]
        h = jnp.concatenate(pieces, axis=0)
        h_last = carry.reshape(1, 1, D_RNN)
    hc_ref[...] = h_last
    hl_ref[0] = h_last

    out = x + _mm(gate * h, wout_ref[0])
    o_ref[...] = out.reshape(bb, tt, D_MODEL)


def _rg_block(x3d, conv_buf, h0, w, layer, bb, tt):
    nb, t, _ = x3d.shape
    j = layer // 2
    tile = pl.BlockSpec((bb, tt, D_MODEL), lambda b, i: (b, i, 0))
    buf_in = pl.BlockSpec((1, bb, CONV_W - 1, D_RNN), lambda b, i: (j, b, 0, 0))
    h_in = pl.BlockSpec((1, bb, 1, D_RNN), lambda b, i: (j, b, 0, 0))
    buf_out = pl.BlockSpec((1, bb, CONV_W - 1, D_RNN), lambda b, i: (0, b, 0, 0))
    h_out = pl.BlockSpec((1, bb, 1, D_RNN), lambda b, i: (0, b, 0, 0))
    vec = _layer_spec((1, D_RNN), j)
    return pl.pallas_call(
        functools.partial(_rg_kernel, bb=bb, tt=tt),
        grid=(nb // bb, t // tt),
        in_specs=[tile, buf_in, h_in, _layer_spec((1, D_MODEL), layer),
                  _layer_spec((D_MODEL, 2 * D_RNN), j), _layer_spec((CONV_W, D_RNN), j), vec,
                  _layer_spec((D_RNN // MXU_DIM, MXU_DIM, 2 * MXU_DIM), j), vec, vec, vec,
                  _layer_spec((D_RNN, D_MODEL), j)],
        out_specs=[tile, buf_out, h_out],
        out_shape=[jax.ShapeDtypeStruct(x3d.shape, F32),
                   jax.ShapeDtypeStruct((1, nb, CONV_W - 1, D_RNN), F32),
                   jax.ShapeDtypeStruct((1, nb, 1, D_RNN), F32)],
        scratch_shapes=[pltpu.VMEM((bb, SUBLANES, D_RNN), F32),
                        pltpu.VMEM((bb, 1, D_RNN), F32)],
        compiler_params=_params(2),
        name="rg_block",
    )(x3d, conv_buf, h0, w["norm_mix_g"], w["rg_w_in"], w["rg_conv_w"], w["rg_conv_b"],
      w["rg_w_gates"], w["rg_b_a"], w["rg_b_x"], w["rg_lambda"], w["rg_w_out"])


def _block_diag_tiles(w):
    per = MXU_DIM // RG_BW
    w4 = w.reshape(RG_BLOCKS // per, per, RG_BW, RG_BW)
    eye = jnp.eye(per, dtype=w.dtype)
    t = w4[:, :, :, None, :] * eye[None, :, None, :, None]
    return t.reshape(RG_BLOCKS // per, MXU_DIM, MXU_DIM)


_GLA_MAIN = 2 * GLA_DK + 2 * GLA_DV


def _gla_kernel(*refs, bb, tt, chunk, slot, aliased):
    (x_ref, s0_ref, g_ref, wmain_ref, wlo_ref, wa2_ref, ba_ref, ng_ref, wout_ref) = refs[:9]
    o_ref, s_ref, oc_ref = refs[10:] if aliased else refs[9:]
    if aliased:
        slot = 0
    n = bb * tt
    n_seg = n // chunk
    assert bb == 1 or tt == chunk
    assert chunk & (chunk - 1) == 0
    shared_state = bb == 1
    i = pl.program_id(1)

    @pl.when(i == 0)
    def _():
        for other in range(s_ref.shape[0]):
            if other != slot:
                s_ref[other] = jnp.zeros(s_ref.shape[1:], F32)
        s_ref[slot] = s0_ref[0]

    x = x_ref[...].reshape(n, D_MODEL)
    xn = _rmsnorm(x, g_ref[0]).astype(BF16)
    proj = jnp.dot(xn, wmain_ref[0], preferred_element_type=F32)
    q3 = (proj[:, :GLA_DK] * (GLA_HK ** -0.5)).reshape(n_seg, chunk, GLA_DK)
    k3 = proj[:, GLA_DK:2 * GLA_DK].reshape(n_seg, chunk, GLA_DK)
    v = proj[:, 2 * GLA_DK:2 * GLA_DK + GLA_DV]
    gsilu = jax.nn.silu(proj[:, 2 * GLA_DK + GLA_DV:])
    a_lo = jnp.dot(xn, wlo_ref[0], preferred_element_type=F32)
    z = _mm(a_lo, wa2_ref[0]) + ba_ref[0]
    la3 = (-_softplus(-z) / GLA_TAU).reshape(n_seg, chunk, GLA_DK)

    early_cast = (lambda t: t.astype(BF16)) if chunk % (2 * SUBLANES) == 0 else (lambda t: t)
    v = early_cast(v)
    t_idx = lax.broadcasted_iota(jnp.int32, (n_seg, chunk, GLA_HK), 1)
    rg = min(n, MXU_DIM)
    row = lax.broadcasted_iota(jnp.int32, (rg, rg), 0)
    col = lax.broadcasted_iota(jnp.int32, (rg, rg), 1)
    same_chunk_causal = (col <= row) & (col >= (row & (-chunk)))

    for h in range(GLA_HEADS):
        kc = slice(h * GLA_HK, (h + 1) * GLA_HK)
        vc = slice(h * GLA_HV, (h + 1) * GLA_HV)
        bcum = la3[:, :, kc]
        s = 1
        while s < chunk:
            bcum = bcum + jnp.where(t_idx >= s, pltpu.roll(bcum, s, 1), 0.0)
            s *= 2
        gl = bcum[:, chunk - 1:chunk, :]
        q_in = early_cast((q3[:, :, kc] * jnp.exp(bcum)).reshape(n, GLA_HK))
        k_in = early_cast((k3[:, :, kc] * jnp.exp(-bcum)).reshape(n, GLA_HK))
        k_end = early_cast((k3[:, :, kc] * jnp.exp(gl - bcum)).reshape(n, GLA_HK))

        for r in range(n // rg):
            rows = slice(r * rg, (r + 1) * rg)
            att = lax.dot_general(q_in[rows].astype(BF16), k_in[rows].astype(BF16),
                                  (((1,), (1,)), ((), ())), preferred_element_type=F32)
            att = jnp.where(same_chunk_causal, att, 0.0).astype(BF16)
            oc_ref[rows, vc] = jnp.dot(att, v[rows, vc].astype(BF16), preferred_element_type=F32)

        state = None
        for c in range(n_seg):
            rows = slice(c * chunk, (c + 1) * chunk)
            sb = 0 if shared_state else c
            if state is None or not shared_state:
                state = s_ref[slot, sb, h]
            oc_ref[rows, vc] = oc_ref[rows, vc] + jnp.dot(
                q_in[rows].astype(BF16), state.astype(BF16), preferred_element_type=F32)
            eg = jnp.exp(jnp.broadcast_to(gl[c], (GLA_HK, GLA_HK))).T
            eg = jnp.concatenate([eg] * (GLA_HV // GLA_HK), axis=1)
            state = eg * state + lax.dot_general(
                k_end[rows].astype(BF16), v[rows, vc].astype(BF16),
                (((0,), (0,)), ((), ())), preferred_element_type=F32)
            if not shared_state or c == n_seg - 1:
                s_ref[slot, sb, h] = state

    o = oc_ref[...]
    heads = []
    for h in range(GLA_HEADS):
        heads.append(_rmsnorm(o[:, h * GLA_HV:(h + 1) * GLA_HV], ng_ref[0]))
    on = jnp.concatenate(heads, axis=1)
    out = x + _mm(on * gsilu, wout_ref[0])
    o_ref[...] = out.reshape(bb, tt, D_MODEL)


def _gla_block(x3d, s0, s_new, w, layer, bb, tt, chunk):
    nb, t, _ = x3d.shape
    n = bb * tt
    j = layer // 2
    tile = pl.BlockSpec((bb, tt, D_MODEL), lambda b, i: (b, i, 0))
    state = pl.BlockSpec((1, bb, GLA_HEADS, GLA_HK, GLA_HV), lambda b, i: (j, b, 0, 0, 0))
    in_specs = [tile, state, _layer_spec((1, D_MODEL), layer), _layer_spec((D_MODEL, _GLA_MAIN), j),
                _layer_spec((D_MODEL, LANES), j), _layer_spec((LANES, GLA_DK), j),
                _layer_spec((1, GLA_DK), j), _layer_spec((1, GLA_HV), j),
                _layer_spec((GLA_DV, D_MODEL), j)]
    args = [x3d, s0, w["norm_mix_g"], w["gla_w_main"], w["gla_w_lo"], w["gla_w_a2"], w["gla_b_a"],
            w["gla_norm_g"], w["gla_w_out"]]
    if s_new is None:
        aliases = {}
        state_out = pl.BlockSpec((s0.shape[0], bb, GLA_HEADS, GLA_HK, GLA_HV),
                                 lambda b, i: (0, b, 0, 0, 0))
    else:
        in_specs.append(pl.BlockSpec(memory_space=pl.ANY))
        args.append(s_new)
        aliases = {len(args) - 1: 1}
        state_out = state
    return pl.pallas_call(
        functools.partial(_gla_kernel, bb=bb, tt=tt, chunk=chunk, slot=j,
                          aliased=s_new is not None),
        grid=(nb // bb, t // tt),
        in_specs=in_specs,
        out_specs=[tile, state_out],
        out_shape=[jax.ShapeDtypeStruct(x3d.shape, F32), jax.ShapeDtypeStruct(s0.shape, F32)],
        input_output_aliases=aliases,
        scratch_shapes=[pltpu.VMEM((n, GLA_DV), F32)],
        compiler_params=_params(2),
        name="gla_block",
    )(*args)


def _run_group(x3d, mem_k, mem_v, rg_h, rg_conv, gla_s, w, tiles):
    nb, t, _ = x3d.shape
    hs, convs = [], []
    s_new = None
    x = x3d
    for layer in range(DEPTH):
        if layer % 2 == 0:
            x, cb, hl = _rg_block(x, rg_conv, rg_h, w, layer, tiles["mix_bb"], tiles["mix_tt"])
            convs.append(cb)
            hs.append(hl)
        else:
            x, s_new = _gla_block(x, gla_s, s_new, w, layer, tiles["gla_bb"], tiles["gla_tt"],
                                  tiles["gla_chunk"])
        x = _xattn(x, w, layer, mem_k, mem_v, tiles["xa_bb"], tiles["xa_tt"])
        x = _mlp(x.reshape(nb * t, D_MODEL), w, layer).reshape(nb, t, D_MODEL)
    h_all = jnp.concatenate(hs, axis=0).reshape(len(hs), nb, D_RNN)
    return x, h_all, jnp.concatenate(convs, axis=0), s_new


def kernel(x_prompt, x_sample, mem_prompt, state_rglru_h, state_rglru_conv, state_gla_S, cache_mem_k, cache_mem_v, norm_mix_g, norm_xa_g, norm_mem_g, norm_mlp_g, final_norm_g, rg_w_in, rg_conv_w, rg_conv_b, rg_w_a, rg_b_a, rg_w_x, rg_b_x, rg_lambda, rg_w_out, gla_w_in, gla_w_a2, gla_b_a, gla_norm_g, gla_w_out, xa_wq, xa_wk, xa_wv, xa_wo, mlp_w1, mlp_w2):
    batch, seq, _ = x_prompt.shape
    dec_batch, dec_seq, _ = x_sample.shape
    n_a = rg_w_in.shape[0]
    n_b = gla_w_in.shape[0]

    def rows(p):
        return p.reshape(p.shape[0], 1, p.shape[1])

    gates = jnp.concatenate(
        [jax.vmap(_block_diag_tiles)(rg_w_a), jax.vmap(_block_diag_tiles)(rg_w_x)], axis=-1)
    lo_pad = LANES - GLA_RANK
    w = dict(
        norm_mix_g=rows(norm_mix_g), norm_xa_g=rows(norm_xa_g), norm_mlp_g=rows(norm_mlp_g),
        final_norm_g=final_norm_g.reshape(1, D_MODEL),
        rg_w_in=rg_w_in.astype(BF16), rg_conv_w=rg_conv_w, rg_conv_b=rows(rg_conv_b),
        rg_w_gates=gates.astype(BF16), rg_b_a=rows(rg_b_a), rg_b_x=rows(rg_b_x),
        rg_lambda=rows(rg_lambda), rg_w_out=rg_w_out.astype(BF16),
        gla_w_main=gla_w_in[:, :, :_GLA_MAIN].astype(BF16),
        gla_w_lo=jnp.pad(gla_w_in[:, :, _GLA_MAIN:], ((0, 0), (0, 0), (0, lo_pad))).astype(BF16),
        gla_w_a2=jnp.pad(gla_w_a2, ((0, 0), (0, lo_pad), (0, 0))).astype(BF16),
        gla_b_a=rows(gla_b_a), gla_norm_g=rows(gla_norm_g), gla_w_out=gla_w_out.astype(BF16),
        xa_wq=xa_wq.astype(BF16), xa_wo=xa_wo.astype(BF16),
        mlp_w1=mlp_w1.astype(BF16), mlp_w2=mlp_w2.astype(BF16),
    )

    mem2d = mem_prompt.reshape(batch * N_MEM, D_MODEL)
    mem_k_prompt, mem_v_prompt, mkb, mvb = _mem_kv(
        mem2d, rows(norm_mem_g), xa_wk.astype(BF16), xa_wv.astype(BF16))
    prompt_tiles = dict(mix_bb=1, mix_tt=512, gla_bb=1, gla_tt=512, gla_chunk=GLA_CHUNK,
                        xa_bb=1, xa_tt=512)
    y_p, h_p, conv_p, s_p = _run_group(
        x_prompt,
        mkb.reshape(DEPTH, batch, N_MEM, D_MODEL), mvb.reshape(DEPTH, batch, N_MEM, D_MODEL),
        jnp.zeros((n_a, batch, 1, D_RNN), F32), jnp.zeros((n_a, batch, CONV_W - 1, D_RNN), F32),
        jnp.zeros((n_b, batch, GLA_HEADS, GLA_HK, GLA_HV), F32), w, prompt_tiles)

    sample_chunk = GLA_CHUNK if dec_seq % GLA_CHUNK == 0 else dec_seq
    sample_tiles = dict(mix_bb=32, mix_tt=dec_seq, gla_bb=8, gla_tt=dec_seq, gla_chunk=sample_chunk,
                        xa_bb=8, xa_tt=dec_seq)
    y_s, h_s, conv_s, s_s = _run_group(
        x_sample, cache_mem_k, cache_mem_v,
        state_rglru_h.reshape(n_a, dec_batch, 1, D_RNN), state_rglru_conv, state_gla_S,
        w, sample_tiles)

    return (y_p, y_s, mem_k_prompt, mem_v_prompt, h_p, conv_p, s_p, h_s, conv_s, s_s)
```

```python
import functools

import jax
import jax.numpy as jnp
from jax import lax
from jax.experimental import pallas as pl
from jax.experimental.pallas import tpu as pltpu

F32 = jnp.float32
BF16 = jnp.bfloat16

D_MODEL = 1024
DEPTH = 4
D_RNN = D_MODEL
RG_BLOCKS = 16
RG_BW = D_RNN // RG_BLOCKS
CONV_W = 4
RG_C = 8.0
GLA_HEADS = 4
GLA_DK = D_MODEL // 2
GLA_DV = D_MODEL
GLA_HK = GLA_DK // GLA_HEADS
GLA_HV = GLA_DV // GLA_HEADS
GLA_RANK = 16
GLA_TAU = 16.0
GLA_CHUNK = 64
N_MEM = 256
XA_HEADS = 4
XA_HD = D_MODEL // XA_HEADS
D_FF = 4 * D_MODEL
EPS = 1e-6

SUBLANES = 8
LANES = 128
MXU_DIM = 256
VMEM_LIMIT_BYTES = 56 * 1024 * 1024
MASKED_SCORE = -1e30
LOG2_E = 1.4426950408889634


def _params(n_axes):
    return pltpu.CompilerParams(
        dimension_semantics=("arbitrary",) * n_axes,
        vmem_limit_bytes=VMEM_LIMIT_BYTES,
    )


def _layer_spec(shape, layer):
    zeros = (0,) * len(shape)
    return pl.BlockSpec((1,) + tuple(shape), lambda *_: (layer,) + zeros,
                        pipeline_mode=pl.Buffered(1))


def _rmsnorm(x, g):
    ms = jnp.mean(x * x, axis=-1, keepdims=True)
    return x * lax.rsqrt(ms + EPS) * g


def _mm(a, w):
    return jnp.dot(a.astype(BF16), w, preferred_element_type=F32)


def _softplus(z):
    return jnp.maximum(z, 0.0) + jnp.log1p(jnp.exp(-jnp.abs(z)))


def _memkv_kernel(mem_ref, g_ref, wk_ref, wv_ref, k_ref, v_ref, kb_ref, vb_ref, *, nbm):
    mn = _rmsnorm(mem_ref[...], g_ref[0]).astype(BF16)
    k = jnp.dot(mn, wk_ref[0], preferred_element_type=F32)
    v = jnp.dot(mn, wv_ref[0], preferred_element_type=F32)
    k_ref[0] = k.reshape(nbm, N_MEM, XA_HEADS, XA_HD)
    v_ref[0] = v.reshape(nbm, N_MEM, XA_HEADS, XA_HD)
    kb_ref[0] = k.astype(BF16)
    vb_ref[0] = v.astype(BF16)


def _mem_kv(mem2d, norm_g, wk, wv, nbm=2):
    n = mem2d.shape[0]
    nb = n // N_MEM
    tm = nbm * N_MEM
    row = pl.BlockSpec((tm, D_MODEL), lambda l, i: (i, 0))
    per_layer = lambda shape: pl.BlockSpec((1,) + shape, lambda l, i: (l, 0, 0))
    out5 = pl.BlockSpec((1, nbm, N_MEM, XA_HEADS, XA_HD), lambda l, i: (l, i, 0, 0, 0))
    out3 = pl.BlockSpec((1, tm, D_MODEL), lambda l, i: (l, i, 0))
    return pl.pallas_call(
        functools.partial(_memkv_kernel, nbm=nbm),
        grid=(DEPTH, n // tm),
        in_specs=[row, per_layer((1, D_MODEL)), per_layer((D_MODEL, D_MODEL)),
                  per_layer((D_MODEL, D_MODEL))],
        out_specs=[out5, out5, out3, out3],
        out_shape=[jax.ShapeDtypeStruct((DEPTH, nb, N_MEM, XA_HEADS, XA_HD), F32)] * 2
        + [jax.ShapeDtypeStruct((DEPTH, n, D_MODEL), BF16)] * 2,
        compiler_params=_params(2),
        name="mem_kv",
    )(mem2d, norm_g, wk, wv)


def _mlp_kernel(x_ref, g_ref, w1_ref, w2_ref, gf_ref, o_ref, *, final, f_chunk):
    x = x_ref[...]
    xn = _rmsnorm(x, g_ref[0]).astype(BF16)
    acc = x
    for c in range(D_FF // f_chunk):
        hid = jnp.dot(xn, w1_ref[0, :, c * f_chunk:(c + 1) * f_chunk], preferred_element_type=F32)
        act = jnp.square(jnp.maximum(hid, 0.0)).astype(BF16)
        acc = acc + jnp.dot(act, w2_ref[0, c * f_chunk:(c + 1) * f_chunk, :], preferred_element_type=F32)
    if final:
        acc = _rmsnorm(acc, gf_ref[...])
    o_ref[...] = acc


def _mlp(x2d, w, layer, tm=512, f_chunk=1024):
    n = x2d.shape[0]
    final = layer == DEPTH - 1
    row = pl.BlockSpec((tm, D_MODEL), lambda i: (i, 0))
    return pl.pallas_call(
        functools.partial(_mlp_kernel, final=final, f_chunk=f_chunk),
        grid=(n // tm,),
        in_specs=[row, _layer_spec((1, D_MODEL), layer), _layer_spec((D_MODEL, D_FF), layer),
                  _layer_spec((D_FF, D_MODEL), layer),
                  pl.BlockSpec((1, D_MODEL), lambda i: (0, 0))],
        out_specs=row,
        out_shape=jax.ShapeDtypeStruct((n, D_MODEL), F32),
        compiler_params=_params(1),
        name="mlp",
    )(x2d, w["norm_mlp_g"], w["mlp_w1"], w["mlp_w2"], w["final_norm_g"])


def _xattn_kernel(x_ref, g_ref, wq_ref, wo_ref, k_ref, v_ref, o_ref, att_ref, *, bb, tt, packed_heads):
    n = bb * tt
    x = x_ref[...].reshape(n, D_MODEL)
    xn = _rmsnorm(x, g_ref[0])
    q = _mm(xn, wq_ref[0])
    scale = XA_HD ** -0.5

    def softmax(s):
        e = jnp.exp(s - jnp.max(s, axis=-1, keepdims=True))
        return (e / jnp.sum(e, axis=-1, keepdims=True)).astype(BF16)

    nt_dims = (((1,), (1,)), ((), ()))
    if packed_heads:
        rows_b = XA_HEADS * tt
        shape = (bb * rows_b, N_MEM * XA_HEADS)
        q_head = (lax.broadcasted_iota(jnp.int32, shape, 0) % rows_b) // tt
        kv_head = lax.broadcasted_iota(jnp.int32, shape, 1) % XA_HEADS
        same_head = q_head == kv_head
        scores = []
        for b in range(bb):
            qs = jnp.concatenate(
                [q[b * tt:(b + 1) * tt, h * XA_HD:(h + 1) * XA_HD] for h in range(XA_HEADS)],
                axis=0).astype(BF16)
            kf = k_ref[0, b].reshape(N_MEM * XA_HEADS, XA_HD).astype(BF16)
            scores.append(lax.dot_general(qs, kf, nt_dims, preferred_element_type=F32))
        s = jnp.concatenate(scores, axis=0) * scale
        p = softmax(jnp.where(same_head, s, MASKED_SCORE))
        for b in range(bb):
            vf = v_ref[0, b].reshape(N_MEM * XA_HEADS, XA_HD).astype(BF16)
            o = jnp.dot(p[b * rows_b:(b + 1) * rows_b], vf, preferred_element_type=F32)
            for h in range(XA_HEADS):
                att_ref[b * tt:(b + 1) * tt, h * XA_HD:(h + 1) * XA_HD] = o[h * tt:(h + 1) * tt]
    else:
        qb = q.astype(BF16)
        for b in range(bb):
            rows = slice(b * tt, (b + 1) * tt)
            head_cols = [slice(h * XA_HD, (h + 1) * XA_HD) for h in range(XA_HEADS)]
            s = jnp.concatenate(
                [lax.dot_general(qb[rows, cols], k_ref[0, b, :, cols], nt_dims,
                                 preferred_element_type=F32) for cols in head_cols], axis=0)
            p = softmax(s * scale)
            for h, cols in enumerate(head_cols):
                att_ref[rows, cols] = jnp.dot(
                    p[h * tt:(h + 1) * tt], v_ref[0, b, :, cols], preferred_element_type=F32)
    out = x + _mm(att_ref[...], wo_ref[0])
    o_ref[...] = out.reshape(bb, tt, D_MODEL)


def _xattn(x3d, w, layer, k, v, bb, tt):
    nb, t, _ = x3d.shape
    packed_heads = k.ndim == 5
    tile = pl.BlockSpec((bb, tt, D_MODEL), lambda b, i: (b, i, 0))
    if packed_heads:
        kv = pl.BlockSpec((1, bb, N_MEM, XA_HEADS, XA_HD), lambda b, i: (layer, b, 0, 0, 0))
    else:
        kv = pl.BlockSpec((1, bb, N_MEM, D_MODEL), lambda b, i: (layer, b, 0, 0))
    return pl.pallas_call(
        functools.partial(_xattn_kernel, bb=bb, tt=tt, packed_heads=packed_heads),
        grid=(nb // bb, t // tt),
        in_specs=[tile, _layer_spec((1, D_MODEL), layer), _layer_spec((D_MODEL, D_MODEL), layer),
                  _layer_spec((D_MODEL, D_MODEL), layer), kv, kv],
        out_specs=tile,
        out_shape=jax.ShapeDtypeStruct(x3d.shape, F32),
        scratch_shapes=[pltpu.VMEM((bb * tt, D_MODEL), F32)],
        compiler_params=_params(2),
        name="xattn",
    )(x3d, w["norm_xa_g"], w["xa_wq"], w["xa_wo"], k, v)


def _rg_kernel(x_ref, buf_ref, h0_ref, g_ref, win_ref, cw_ref, cb_ref, wg_ref, ba_ref, bx_ref,
               lam_ref, wout_ref, o_ref, nbuf_ref, hl_ref, u_ref, hc_ref, tm_ref, *, bb, tt):
    n = bb * tt
    pad = SUBLANES
    n_prev = CONV_W - 1
    time_major = bb == SUBLANES and tt > SUBLANES
    i = pl.program_id(1)

    @pl.when(i == 0)
    def _():
        u_ref[...] = jnp.zeros((bb, pad, D_RNN), F32)
        if time_major:
            for j in range(n_prev):
                u_ref[pad - n_prev + j] = buf_ref[0, :, j, :]
        else:
            u_ref[:, pad - n_prev:pad, :] = buf_ref[0]
        hc_ref[...] = h0_ref[0]

    x = x_ref[...].reshape(n, D_MODEL)
    xn = _rmsnorm(x, g_ref[0])
    yx = _mm(xn, win_ref[0])
    gate = jax.nn.gelu(yx[:, :D_RNN])
    u3 = yx[:, D_RNN:].reshape(bb, tt, D_RNN)

    def tap(j):
        return cw_ref[0, CONV_W - 1 - j:CONV_W - j, :]

    if time_major:
        tm_ref[...] = u3
        slabs = [u_ref[pad - n_prev + j] for j in range(n_prev)]
        slabs += [tm_ref[:, t, :] for t in range(tt)]
        xc_slabs = []
        for t in range(tt):
            acc = cb_ref[0] + tap(0) * slabs[n_prev + t]
            for j in range(1, CONV_W):
                acc = acc + tap(j) * slabs[n_prev + t - j]
            xc_slabs.append(acc)
        xc2 = jnp.concatenate(xc_slabs, axis=0)
        for j in range(n_prev):
            u_ref[pad - n_prev + j] = slabs[tt + j]
            nbuf_ref[0, :, j, :] = slabs[tt + j]
    else:
        ext = jnp.concatenate([u_ref[...], u3], axis=1)
        xc = cb_ref[0].reshape(1, 1, D_RNN) + tap(0).reshape(1, 1, D_RNN) * u3
        for j in range(1, CONV_W):
            xc = xc + tap(j).reshape(1, 1, D_RNN) * pltpu.roll(ext, j, 1)[:, pad:, :]
        nbuf_ref[0] = ext[:, pad + tt - n_prev:, :]
        u_ref[...] = ext[:, tt:, :]
        xc2 = xc.reshape(n, D_RNN)

    gates =[_mm(xc2[:, c * MXU_DIM:(c + 1) * MXU_DIM], wg_ref[0, c]) for c in range(D_RNN // MXU_DIM)]
    r = jax.nn.sigmoid(jnp.concatenate([gc[:, :MXU_DIM] for gc in gates], axis=1) + ba_ref[0])
    ig = jax.nn.sigmoid(jnp.concatenate([gc[:, MXU_DIM:] for gc in gates], axis=1) + bx_ref[0])
    log_a = r * ((-RG_C) * _softplus(-lam_ref[0]))
    a = jnp.exp(log_a)
    m2 = -jnp.tanh(log_a) * (a * a + 1.0)
    mult = jnp.where(m2 > 0.0, m2 * lax.rsqrt(m2), 0.0)
    bt = mult * (ig * xc2)

    if time_major:
        carry = hc_ref[:, 0, :]
        for t in range(tt):
            rows = slice(t * bb, (t + 1) * bb)
            carry = a[rows] * carry + bt[rows]
            tm_ref[:, t, :] = carry
        h = tm_ref[...].reshape(n, D_RNN)
        hc_ref[:, 0, :] = carry
        hl_ref[0, :, 0, :] = carry
    else:
        groups = n // SUBLANES
        ag = a.reshape(groups, SUBLANES, D_RNN)
        bg = bt.reshape(groups, SUBLANES, D_RNN)
        t_idx = lax.broadcasted_iota(jnp.int32, (groups, SUBLANES, D_RNN), 1)
        s = 1
        while s < SUBLANES:
            keep = t_idx >= s
            a_s = jnp.where(keep, pltpu.roll(ag, s, 1), 1.0)
            b_s = jnp.where(keep, pltpu.roll(bg, s, 1), 0.0)
            bg = ag * b_s + bg
            ag = ag * a_s
            s *= 2
        if tt == SUBLANES:
            h = (ag * hc_ref[...] + bg).reshape(n, D_RNN)
            h_last = h.reshape(bb, tt, D_RNN)[:, tt - 1:tt, :]
        else:
            assert bb == 1
            carry = hc_ref[0]
            pieces = []
            for gi in range(groups):
                hg = ag[gi] * carry + bg[gi]
                pieces.append(hg)
                carry = hg[SUBLANES - 1:SUBLANES, :]
            h = jnp.concatenate(pieces, axis=0)
            h_last = carry.reshape(1, 1, D_RNN)
        hc_ref[...] = h_last
        hl_ref[0] = h_last

    out = x + _mm(gate * h, wout_ref[0])
    o_ref[...] = out.reshape(bb, tt, D_MODEL)


def _rg_block(x3d, conv_buf, h0, w, layer, bb, tt):
    nb, t, _ = x3d.shape
    j = layer // 2
    tile = pl.BlockSpec((bb, tt, D_MODEL), lambda b, i: (b, i, 0))
    buf_in = pl.BlockSpec((1, bb, CONV_W - 1, D_RNN), lambda b, i: (j, b, 0, 0))
    h_in = pl.BlockSpec((1, bb, 1, D_RNN), lambda b, i: (j, b, 0, 0))
    buf_out = pl.BlockSpec((1, bb, CONV_W - 1, D_RNN), lambda b, i: (0, b, 0, 0))
    h_out = pl.BlockSpec((1, bb, 1, D_RNN), lambda b, i: (0, b, 0, 0))
    vec = _layer_spec((1, D_RNN), j)
    return pl.pallas_call(
        functools.partial(_rg_kernel, bb=bb, tt=tt),
        grid=(nb // bb, t // tt),
        in_specs=[tile, buf_in, h_in, _layer_spec((1, D_MODEL), layer),
                  _layer_spec((D_MODEL, 2 * D_RNN), j), _layer_spec((CONV_W, D_RNN), j), vec,
                  _layer_spec((D_RNN // MXU_DIM, MXU_DIM, 2 * MXU_DIM), j), vec, vec, vec,
                  _layer_spec((D_RNN, D_MODEL), j)],
        out_specs=[tile, buf_out, h_out],
        out_shape=[jax.ShapeDtypeStruct(x3d.shape, F32),
                   jax.ShapeDtypeStruct((1, nb, CONV_W - 1, D_RNN), F32),
                   jax.ShapeDtypeStruct((1, nb, 1, D_RNN), F32)],
        scratch_shapes=[pltpu.VMEM((bb, SUBLANES, D_RNN), F32),
                        pltpu.VMEM((bb, 1, D_RNN), F32),
                        pltpu.VMEM((bb, tt, D_RNN) if bb == SUBLANES and tt > SUBLANES
                                   else (1, SUBLANES, LANES), F32)],
        compiler_params=_params(2),
        name="rg_block",
    )(x3d, conv_buf, h0, w["norm_mix_g"], w["rg_w_in"], w["rg_conv_w"], w["rg_conv_b"],
      w["rg_w_gates"], w["rg_b_a"], w["rg_b_x"], w["rg_lambda"], w["rg_w_out"])


def _block_diag_tiles(w):
    per = MXU_DIM // RG_BW
    w4 = w.reshape(RG_BLOCKS // per, per, RG_BW, RG_BW)
    eye = jnp.eye(per, dtype=w.dtype)
    t = w4[:, :, :, None, :] * eye[None, :, None, :, None]
    return t.reshape(RG_BLOCKS // per, MXU_DIM, MXU_DIM)


_GLA_MAIN = 2 * GLA_DK + 2 * GLA_DV


def _gla_kernel(*refs, bb, tt, chunk, slot, aliased):
    (x_ref, s0_ref, g_ref, wmain_ref, wlo_ref, wa2_ref, ba_ref, ng_ref, wout_ref) = refs[:9]
    o_ref, s_ref, oc_ref = refs[10:] if aliased else refs[9:]
    if aliased:
        slot = 0
    n = bb * tt
    n_seg = n // chunk
    assert bb == 1 or tt == chunk
    assert chunk & (chunk - 1) == 0
    shared_state = bb == 1
    i = pl.program_id(1)

    @pl.when(i == 0)
    def _():
        for other in range(s_ref.shape[0]):
            if other != slot:
                s_ref[other] = jnp.zeros(s_ref.shape[1:], F32)
        s_ref[slot] = s0_ref[0]

    x = x_ref[...].reshape(n, D_MODEL)
    xn = _rmsnorm(x, g_ref[0]).astype(BF16)
    proj = jnp.dot(xn, wmain_ref[0], preferred_element_type=F32)
    q3 = (proj[:, :GLA_DK] * (GLA_HK ** -0.5)).reshape(n_seg, chunk, GLA_DK)
    k3 = proj[:, GLA_DK:2 * GLA_DK].reshape(n_seg, chunk, GLA_DK)
    v = proj[:, 2 * GLA_DK:2 * GLA_DK + GLA_DV]
    gsilu = jax.nn.silu(proj[:, 2 * GLA_DK + GLA_DV:])
    a_lo = jnp.dot(xn, wlo_ref[0], preferred_element_type=F32)
    z = _mm(a_lo, wa2_ref[0]) + ba_ref[0]
    nz = -z
    softplus_nz = jnp.maximum(nz, 0.0) + jnp.log(1.0 + jnp.exp(-jnp.abs(nz)))
    la3 = (softplus_nz * (-LOG2_E / GLA_TAU)).reshape(n_seg, chunk, GLA_DK)

    early_cast = (lambda t: t.astype(BF16)) if chunk % (2 * SUBLANES) == 0 else (lambda t: t)
    v = early_cast(v)
    t_idx = lax.broadcasted_iota(jnp.int32, (n_seg, chunk, GLA_HK), 1)
    rg = min(n, MXU_DIM)
    row = lax.broadcasted_iota(jnp.int32, (rg, rg), 0)
    col = lax.broadcasted_iota(jnp.int32, (rg, rg), 1)
    same_chunk_causal = (col <= row) & (col >= (row & (-chunk)))

    for h in range(GLA_HEADS):
        kc = slice(h * GLA_HK, (h + 1) * GLA_HK)
        vc = slice(h * GLA_HV, (h + 1) * GLA_HV)
        bcum = la3[:, :, kc]
        s = 1
        while s < chunk:
            bcum = bcum + jnp.where(t_idx >= s, pltpu.roll(bcum, s, 1), 0.0)
            s *= 2
        gl = bcum[:, chunk - 1:chunk, :]
        q_in = early_cast((q3[:, :, kc] * jnp.exp2(bcum)).reshape(n, GLA_HK))
        k_dec = k3[:, :, kc] * jnp.exp2(-bcum)
        k_in = early_cast(k_dec.reshape(n, GLA_HK))
        k_end = early_cast((k_dec * jnp.exp2(gl)).reshape(n, GLA_HK))

        for r in range(n // rg):
            rows = slice(r * rg, (r + 1) * rg)
            att = lax.dot_general(q_in[rows].astype(BF16), k_in[rows].astype(BF16),
                                  (((1,), (1,)), ((), ())), preferred_element_type=F32)
            att = jnp.where(same_chunk_causal, att, 0.0).astype(BF16)
            oc_ref[rows, vc] = jnp.dot(att, v[rows, vc].astype(BF16), preferred_element_type=F32)

        state = None
        for c in range(n_seg):
            rows = slice(c * chunk, (c + 1) * chunk)
            sb = 0 if shared_state else c
            if state is None or not shared_state:
                state = s_ref[slot, sb, h]
            oc_ref[rows, vc] = oc_ref[rows, vc] + jnp.dot(
                q_in[rows].astype(BF16), state.astype(BF16), preferred_element_type=F32)
            eg = jnp.exp2(jnp.broadcast_to(gl[c], (GLA_HK, GLA_HK))).T
            eg = jnp.concatenate([eg] * (GLA_HV // GLA_HK), axis=1)
            state = eg * state + lax.dot_general(
                k_end[rows].astype(BF16), v[rows, vc].astype(BF16),
                (((0,), (0,)), ((), ())), preferred_element_type=F32)
            if not shared_state or c == n_seg - 1:
                s_ref[slot, sb, h] = state

    o = oc_ref[...]
    heads = []
    for h in range(GLA_HEADS):
        heads.append(_rmsnorm(o[:, h * GLA_HV:(h + 1) * GLA_HV], ng_ref[0]))
    on = jnp.concatenate(heads, axis=1)
    out = x + _mm(on * gsilu, wout_ref[0])
    o_ref[...] = out.reshape(bb, tt, D_MODEL)


def _gla_block(x3d, s0, s_new, w, layer, bb, tt, chunk):
    nb, t, _ = x3d.shape
    n = bb * tt
    j = layer // 2
    tile = pl.BlockSpec((bb, tt, D_MODEL), lambda b, i: (b, i, 0))
    state = pl.BlockSpec((1, bb, GLA_HEADS, GLA_HK, GLA_HV), lambda b, i: (j, b, 0, 0, 0))
    in_specs = [tile, state, _layer_spec((1, D_MODEL), layer), _layer_spec((D_MODEL, _GLA_MAIN), j),
                _layer_spec((D_MODEL, LANES), j), _layer_spec((LANES, GLA_DK), j),
                _layer_spec((1, GLA_DK), j), _layer_spec((1, GLA_HV), j),
                _layer_spec((GLA_DV, D_MODEL), j)]
    args = [x3d, s0, w["norm_mix_g"], w["gla_w_main"], w["gla_w_lo"], w["gla_w_a2"], w["gla_b_a"],
            w["gla_norm_g"], w["gla_w_out"]]
    if s_new is None:
        aliases = {}
        state_out = pl.BlockSpec((s0.shape[0], bb, GLA_HEADS, GLA_HK, GLA_HV),
                                 lambda b, i: (0, b, 0, 0, 0))
    else:
        in_specs.append(pl.BlockSpec(memory_space=pl.ANY))
        args.append(s_new)
        aliases = {len(args) - 1: 1}
        state_out = state
    return pl.pallas_call(
        functools.partial(_gla_kernel, bb=bb, tt=tt, chunk=chunk, slot=j,
                          aliased=s_new is not None),
        grid=(nb // bb, t // tt),
        in_specs=in_specs,
        out_specs=[tile, state_out],
        out_shape=[jax.ShapeDtypeStruct(x3d.shape, F32), jax.ShapeDtypeStruct(s0.shape, F32)],
        input_output_aliases=aliases,
        scratch_shapes=[pltpu.VMEM((n, GLA_DV), F32)],
        compiler_params=_params(2),
        name="gla_block",
    )(*args)


def _run_group(x3d, mem_k, mem_v, rg_h, rg_conv, gla_s, w, tiles):
    nb, t, _ = x3d.shape
    hs, convs = [], []
    s_new = None
    x = x3d
    for layer in range(DEPTH):
        if layer % 2 == 0:
            x, cb, hl = _rg_block(x, rg_conv, rg_h, w, layer, tiles["mix_bb"], tiles["mix_tt"])
            convs.append(cb)
            hs.append(hl)
        else:
            x, s_new = _gla_block(x, gla_s, s_new, w, layer, tiles["gla_bb"], tiles["gla_tt"],
                                  tiles["gla_chunk"])
        x = _xattn(x, w, layer, mem_k, mem_v, tiles["xa_bb"], tiles["xa_tt"])
        x = _mlp(x.reshape(nb * t, D_MODEL), w, layer).reshape(nb, t, D_MODEL)
    h_all = jnp.concatenate(hs, axis=0).reshape(len(hs), nb, D_RNN)
    return x, h_all, jnp.concatenate(convs, axis=0), s_new


def kernel(x_prompt, x_sample, mem_prompt, state_rglru_h, state_rglru_conv, state_gla_S, cache_mem_k, cache_mem_v, norm_mix_g, norm_xa_g, norm_mem_g, norm_mlp_g, final_norm_g, rg_w_in, rg_conv_w, rg_conv_b, rg_w_a, rg_b_a, rg_w_x, rg_b_x, rg_lambda, rg_w_out, gla_w_in, gla_w_a2, gla_b_a, gla_norm_g, gla_w_out, xa_wq, xa_wk, xa_wv, xa_wo, mlp_w1, mlp_w2):
    batch, seq, _ = x_prompt.shape
    dec_batch, dec_seq, _ = x_sample.shape
    n_a = rg_w_in.shape[0]
    n_b = gla_w_in.shape[0]

    def rows(p):
        return p.reshape(p.shape[0], 1, p.shape[1])

    gates = jnp.concatenate(
        [jax.vmap(_block_diag_tiles)(rg_w_a), jax.vmap(_block_diag_tiles)(rg_w_x)], axis=-1)
    lo_pad = LANES - GLA_RANK
    w = dict(
        norm_mix_g=rows(norm_mix_g), norm_xa_g=rows(norm_xa_g), norm_mlp_g=rows(norm_mlp_g),
        final_norm_g=final_norm_g.reshape(1, D_MODEL),
        rg_w_in=rg_w_in.astype(BF16), rg_conv_w=rg_conv_w, rg_conv_b=rows(rg_conv_b),
        rg_w_gates=gates.astype(BF16), rg_b_a=rows(rg_b_a), rg_b_x=rows(rg_b_x),
        rg_lambda=rows(rg_lambda), rg_w_out=rg_w_out.astype(BF16),
        gla_w_main=gla_w_in[:, :, :_GLA_MAIN].astype(BF16),
        gla_w_lo=jnp.pad(gla_w_in[:, :, _GLA_MAIN:], ((0, 0), (0, 0), (0, lo_pad))).astype(BF16),
        gla_w_a2=jnp.pad(gla_w_a2, ((0, 0), (0, lo_pad), (0, 0))).astype(BF16),
        gla_b_a=rows(gla_b_a), gla_norm_g=rows(gla_norm_g), gla_w_out=gla_w_out.astype(BF16),
        xa_wq=xa_wq.astype(BF16), xa_wo=xa_wo.astype(BF16),
        mlp_w1=mlp_w1.astype(BF16), mlp_w2=mlp_w2.astype(BF16),
    )

    mem2d = mem_prompt.reshape(batch * N_MEM, D_MODEL)
    mem_k_prompt, mem_v_prompt, mkb, mvb = _mem_kv(
        mem2d, rows(norm_mem_g), xa_wk.astype(BF16), xa_wv.astype(BF16))
    rg_bb, rg_tt = (SUBLANES, 512 // SUBLANES) if batch % SUBLANES == 0 else (1, 512)
    prompt_tiles = dict(mix_bb=rg_bb, mix_tt=rg_tt, gla_bb=1, gla_tt=512, gla_chunk=GLA_CHUNK,
                        xa_bb=1, xa_tt=512)
    y_p, h_p, conv_p, s_p = _run_group(
        x_prompt,
        mkb.reshape(DEPTH, batch, N_MEM, D_MODEL), mvb.reshape(DEPTH, batch, N_MEM, D_MODEL),
        jnp.zeros((n_a, batch, 1, D_RNN), F32), jnp.zeros((n_a, batch, CONV_W - 1, D_RNN), F32),
        jnp.zeros((n_b, batch, GLA_HEADS, GLA_HK, GLA_HV), F32), w, prompt_tiles)

    sample_chunk = GLA_CHUNK if dec_seq % GLA_CHUNK == 0 else dec_seq
    sample_tiles = dict(mix_bb=32, mix_tt=dec_seq, gla_bb=8, gla_tt=dec_seq, gla_chunk=sample_chunk,
                        xa_bb=8, xa_tt=dec_seq)
    y_s, h_s, conv_s, s_s = _run_group(
        x_sample, cache_mem_k, cache_mem_v,
        state_rglru_h.reshape(n_a, dec_batch, 1, D_RNN), state_rglru_conv, state_gla_S,
        w, sample_tiles)

    return (y_p, y_s, mem_k_prompt, mem_v_prompt, h_p, conv_p, s_p, h_s, conv_s, s_s)
```

```python
import functools
from typing import NamedTuple

import jax
import jax.numpy as jnp
from jax import lax
from jax.experimental import pallas as pl
from jax.experimental.pallas import tpu as pltpu

F32 = jnp.float32
BF16 = jnp.bfloat16

D_MODEL = 1024
DEPTH = 4
D_RNN = D_MODEL
RG_BLOCKS = 16
RG_BW = D_RNN // RG_BLOCKS
CONV_W = 4
RG_C = 8.0
GLA_HEADS = 4
GLA_DK = D_MODEL // 2
GLA_DV = D_MODEL
GLA_HK = GLA_DK // GLA_HEADS
GLA_HV = GLA_DV // GLA_HEADS
GLA_RANK = 16
GLA_TAU = 16.0
GLA_CHUNK = 64
N_MEM = 256
XA_HEADS = 4
XA_HD = D_MODEL // XA_HEADS
D_FF = 4 * D_MODEL
EPS = 1e-6

SUBLANES = 8
LANES = 128
MXU_DIM = 256
VMEM_LIMIT_BYTES = 56 * 1024 * 1024
MASKED_SCORE = -1e30
LOG2_E = 1.4426950408889634
ROW_TILE = 512


class _Group(NamedTuple):
    nb: int
    t: int

    @property
    def shape(self):
        return (self.nb, self.t, D_MODEL)

    def tile(self, bb, tt):
        return pl.BlockSpec((bb, tt, D_MODEL), lambda b, i: (b, i, 0))

    def grid(self, bb, tt):
        return (self.nb // bb, self.t // tt)


def _params(n_axes):
    return pltpu.CompilerParams(
        dimension_semantics=("arbitrary",) * n_axes,
        vmem_limit_bytes=VMEM_LIMIT_BYTES,
    )


def _layer_spec(shape, layer):
    zeros = (0,) * len(shape)
    return pl.BlockSpec((1,) + tuple(shape), lambda *_: (layer,) + zeros,
                        pipeline_mode=pl.Buffered(1))


def _rmsnorm(x, g):
    ms = jnp.mean(x * x, axis=-1, keepdims=True)
    return x * lax.rsqrt(ms + EPS) * g


def _mm(a, w):
    return jnp.dot(a.astype(BF16), w, preferred_element_type=F32)


def _softplus(z):
    return jnp.maximum(z, 0.0) + jnp.log1p(jnp.exp(-jnp.abs(z)))


def _memkv_kernel(mem_ref, g_ref, wk_ref, wv_ref, k_ref, v_ref, kb_ref, vb_ref, *, nbm):
    mn = _rmsnorm(mem_ref[...], g_ref[0]).astype(BF16)
    k = jnp.dot(mn, wk_ref[0], preferred_element_type=F32)
    v = jnp.dot(mn, wv_ref[0], preferred_element_type=F32)
    k_ref[0] = k.reshape(nbm, N_MEM, XA_HEADS, XA_HD)
    v_ref[0] = v.reshape(nbm, N_MEM, XA_HEADS, XA_HD)
    kb_ref[0] = k.astype(BF16)
    vb_ref[0] = v.astype(BF16)


def _mem_kv(mem2d, norm_g, wk, wv, nbm=2):
    n = mem2d.shape[0]
    nb = n // N_MEM
    tm = nbm * N_MEM
    row = pl.BlockSpec((tm, D_MODEL), lambda l, i: (i, 0))
    per_layer = lambda shape: pl.BlockSpec((1,) + shape, lambda l, i: (l, 0, 0))
    out5 = pl.BlockSpec((1, nbm, N_MEM, XA_HEADS, XA_HD), lambda l, i: (l, i, 0, 0, 0))
    out3 = pl.BlockSpec((1, tm, D_MODEL), lambda l, i: (l, i, 0))
    return pl.pallas_call(
        functools.partial(_memkv_kernel, nbm=nbm),
        grid=(DEPTH, n // tm),
        in_specs=[row, per_layer((1, D_MODEL)), per_layer((D_MODEL, D_MODEL)),
                  per_layer((D_MODEL, D_MODEL))],
        out_specs=[out5, out5, out3, out3],
        out_shape=[jax.ShapeDtypeStruct((DEPTH, nb, N_MEM, XA_HEADS, XA_HD), F32)] * 2
        + [jax.ShapeDtypeStruct((DEPTH, n, D_MODEL), BF16)] * 2,
        compiler_params=_params(2),
        name="mem_kv",
    )(mem2d, norm_g, wk, wv)


def _mlp_kernel(xa_ref, xb_ref, g_ref, w1_ref, w2_ref, gf_ref, oa_ref, ob_ref, *,
                steps_a, final, f_chunk):
    def apply(x_ref, o_ref):
        x = x_ref[...].reshape(-1, D_MODEL)
        xn = _rmsnorm(x, g_ref[0]).astype(BF16)
        acc = x
        for c in range(D_FF // f_chunk):
            cols = slice(c * f_chunk, (c + 1) * f_chunk)
            hid = jnp.dot(xn, w1_ref[0, :, cols], preferred_element_type=F32)
            act = jnp.square(jnp.maximum(hid, 0.0)).astype(BF16)
            acc = acc + jnp.dot(act, w2_ref[0, cols, :], preferred_element_type=F32)
        if final:
            acc = _rmsnorm(acc, gf_ref[...])
        o_ref[...] = acc.reshape(o_ref.shape)

    p = pl.program_id(0)
    pl.when(p < steps_a)(lambda: apply(xa_ref, oa_ref))
    pl.when(p >= steps_a)(lambda: apply(xb_ref, ob_ref))


def _mlp(xa, group_a, tile_a, xb, group_b, tile_b, w, layer, f_chunk=1024):
    final = layer == DEPTH - 1
    t_steps_a = group_a.grid(*tile_a)[1]
    t_steps_b = group_b.grid(*tile_b)[1]
    steps_a = group_a.grid(*tile_a)[0] * t_steps_a
    steps_b = group_b.grid(*tile_b)[0] * t_steps_b

    def index_a(p):
        q = jnp.minimum(p, steps_a - 1)
        return (q // t_steps_a, q % t_steps_a, 0)

    def index_b(p):
        q = jnp.maximum(p - steps_a, 0)
        return (q // t_steps_b, q % t_steps_b, 0)

    spec_a = pl.BlockSpec(tuple(tile_a) + (D_MODEL,), index_a)
    spec_b = pl.BlockSpec(tuple(tile_b) + (D_MODEL,), index_b)
    return pl.pallas_call(
        functools.partial(_mlp_kernel, steps_a=steps_a, final=final, f_chunk=f_chunk),
        grid=(steps_a + steps_b,),
        in_specs=[spec_a, spec_b, _layer_spec((1, D_MODEL), layer),
                  _layer_spec((D_MODEL, D_FF), layer), _layer_spec((D_FF, D_MODEL), layer),
                  pl.BlockSpec((1, D_MODEL), lambda p: (0, 0))],
        out_specs=[spec_a, spec_b],
        out_shape=[jax.ShapeDtypeStruct(group_a.shape, F32),
                   jax.ShapeDtypeStruct(group_b.shape, F32)],
        compiler_params=_params(1),
        name="mlp",
    )(xa, xb, w["norm_mlp_g"], w["mlp_w1"], w["mlp_w2"], w["final_norm_g"])


def _xattn_kernel(x_ref, g_ref, wq_ref, wo_ref, k_ref, v_ref, o_ref, att_ref, *, bb, tt, packed_heads):
    n = bb * tt
    x = x_ref[...].reshape(n, D_MODEL)
    xn = _rmsnorm(x, g_ref[0])
    q = _mm(xn, wq_ref[0])
    scale = XA_HD ** -0.5

    def softmax(s):
        e = jnp.exp(s - jnp.max(s, axis=-1, keepdims=True))
        return (e / jnp.sum(e, axis=-1, keepdims=True)).astype(BF16)

    nt_dims = (((1,), (1,)), ((), ()))
    if packed_heads:
        rows_b = XA_HEADS * tt
        shape = (bb * rows_b, N_MEM * XA_HEADS)
        q_head = (lax.broadcasted_iota(jnp.int32, shape, 0) % rows_b) // tt
        kv_head = lax.broadcasted_iota(jnp.int32, shape, 1) % XA_HEADS
        same_head = q_head == kv_head
        scores = []
        for b in range(bb):
            qs = jnp.concatenate(
                [q[b * tt:(b + 1) * tt, h * XA_HD:(h + 1) * XA_HD] for h in range(XA_HEADS)],
                axis=0).astype(BF16)
            kf = k_ref[0, b].reshape(N_MEM * XA_HEADS, XA_HD).astype(BF16)
            scores.append(lax.dot_general(qs, kf, nt_dims, preferred_element_type=F32))
        s = jnp.concatenate(scores, axis=0) * scale
        p = softmax(jnp.where(same_head, s, MASKED_SCORE))
        for b in range(bb):
            vf = v_ref[0, b].reshape(N_MEM * XA_HEADS, XA_HD).astype(BF16)
            o = jnp.dot(p[b * rows_b:(b + 1) * rows_b], vf, preferred_element_type=F32)
            for h in range(XA_HEADS):
                att_ref[b * tt:(b + 1) * tt, h * XA_HD:(h + 1) * XA_HD] = o[h * tt:(h + 1) * tt]
    else:
        qb = q.astype(BF16)
        for b in range(bb):
            rows = slice(b * tt, (b + 1) * tt)
            head_cols = [slice(h * XA_HD, (h + 1) * XA_HD) for h in range(XA_HEADS)]
            s = jnp.concatenate(
                [lax.dot_general(qb[rows, cols], k_ref[0, b, :, cols], nt_dims,
                                 preferred_element_type=F32) for cols in head_cols], axis=0)
            p = softmax(s * scale)
            for h, cols in enumerate(head_cols):
                att_ref[rows, cols] = jnp.dot(
                    p[h * tt:(h + 1) * tt], v_ref[0, b, :, cols], preferred_element_type=F32)
    out = x + _mm(att_ref[...], wo_ref[0])
    o_ref[...] = out.reshape(o_ref.shape)


def _xattn(x, group, w, layer, k, v, bb, tt):
    packed_heads = k.ndim == 5
    if packed_heads:
        kv = pl.BlockSpec((1, bb, N_MEM, XA_HEADS, XA_HD), lambda b, i: (layer, b, 0, 0, 0))
    else:
        kv = pl.BlockSpec((1, bb, N_MEM, D_MODEL), lambda b, i: (layer, b, 0, 0))
    return pl.pallas_call(
        functools.partial(_xattn_kernel, bb=bb, tt=tt, packed_heads=packed_heads),
        grid=group.grid(bb, tt),
        in_specs=[group.tile(bb, tt), _layer_spec((1, D_MODEL), layer),
                  _layer_spec((D_MODEL, D_MODEL), layer), _layer_spec((D_MODEL, D_MODEL), layer),
                  kv, kv],
        out_specs=group.tile(bb, tt),
        out_shape=jax.ShapeDtypeStruct(group.shape, F32),
        scratch_shapes=[pltpu.VMEM((bb * tt, D_MODEL), F32)],
        compiler_params=_params(2),
        name="xattn",
    )(x, w["norm_xa_g"], w["xa_wq"], w["xa_wo"], k, v)


def _rg_kernel(x_ref, buf_ref, h0_ref, g_ref, win_ref, cw_ref, cb_ref, wg_ref, ba_ref, bx_ref,
               lam_ref, wout_ref, o_ref, nbuf_ref, hl_ref, u_ref, hc_ref, tm_ref, *, bb, tt, slabs):
    n = bb * tt
    pad = SUBLANES
    n_prev = CONV_W - 1
    assert (bb == SUBLANES) if slabs else (tt == SUBLANES)
    i = pl.program_id(1)

    @pl.when(i == 0)
    def _():
        u_ref[...] = jnp.zeros((bb, pad, D_RNN), F32)
        if slabs:
            for j in range(n_prev):
                u_ref[pad - n_prev + j] = buf_ref[0, :, j, :]
        else:
            u_ref[:, pad - n_prev:pad, :] = buf_ref[0]
        hc_ref[...] = h0_ref[0]

    if slabs:
        x = jnp.concatenate([x_ref[:, t, :] for t in range(tt)], axis=0)
    else:
        x = x_ref[...].reshape(n, D_MODEL)
    xn = _rmsnorm(x, g_ref[0])
    yx = _mm(xn, win_ref[0])
    gate = jax.nn.gelu(yx[:, :D_RNN])
    u = yx[:, D_RNN:]

    def tap(j):
        return cw_ref[0, CONV_W - 1 - j:CONV_W - j, :]

    if slabs:
        steps = [u_ref[pad - n_prev + j] for j in range(n_prev)]
        steps += [u[t * bb:(t + 1) * bb] for t in range(tt)]
        xc_steps = []
        for t in range(tt):
            acc = cb_ref[0] + tap(0) * steps[n_prev + t]
            for j in range(1, CONV_W):
                acc = acc + tap(j) * steps[n_prev + t - j]
            xc_steps.append(acc)
        xc2 = jnp.concatenate(xc_steps, axis=0)
        for j in range(n_prev):
            u_ref[pad - n_prev + j] = steps[tt + j]
            nbuf_ref[0, :, j, :] = steps[tt + j]
    else:
        u3 = u.reshape(bb, tt, D_RNN)
        ext = jnp.concatenate([u_ref[...], u3], axis=1)
        xc = cb_ref[0].reshape(1, 1, D_RNN) + tap(0).reshape(1, 1, D_RNN) * u3
        for j in range(1, CONV_W):
            xc = xc + tap(j).reshape(1, 1, D_RNN) * pltpu.roll(ext, j, 1)[:, pad:, :]
        nbuf_ref[0] = ext[:, pad + tt - n_prev:, :]
        u_ref[...] = ext[:, tt:, :]
        xc2 = xc.reshape(n, D_RNN)

    gates = [_mm(xc2[:, c * MXU_DIM:(c + 1) * MXU_DIM], wg_ref[0, c]) for c in range(D_RNN // MXU_DIM)]
    r = jax.nn.sigmoid(jnp.concatenate([gc[:, :MXU_DIM] for gc in gates], axis=1) + ba_ref[0])
    ig = jax.nn.sigmoid(jnp.concatenate([gc[:, MXU_DIM:] for gc in gates], axis=1) + bx_ref[0])
    log_a = r * ((-RG_C) * _softplus(-lam_ref[0]))
    a = jnp.exp(log_a)
    m2 = -jnp.tanh(log_a) * (a * a + 1.0)
    mult = jnp.where(m2 > 0.0, m2 * lax.rsqrt(m2), 0.0)
    bt = mult * (ig * xc2)

    if slabs:
        carry = hc_ref[:, 0, :]
        h_steps = []
        for t in range(tt):
            rows = slice(t * bb, (t + 1) * bb)
            carry = a[rows] * carry + bt[rows]
            h_steps.append(carry)
        h = jnp.concatenate(h_steps, axis=0)
        hc_ref[:, 0, :] = carry
        hl_ref[0, :, 0, :] = carry
    else:
        ag = a.reshape(bb, tt, D_RNN)
        bg = bt.reshape(bb, tt, D_RNN)
        t_idx = lax.broadcasted_iota(jnp.int32, (bb, tt, D_RNN), 1)
        s = 1
        while s < tt:
            keep = t_idx >= s
            a_s = jnp.where(keep, pltpu.roll(ag, s, 1), 1.0)
            b_s = jnp.where(keep, pltpu.roll(bg, s, 1), 0.0)
            bg = ag * b_s + bg
            ag = ag * a_s
            s *= 2
        h3 = ag * hc_ref[...] + bg
        h = h3.reshape(n, D_RNN)
        hc_ref[...] = h3[:, tt - 1:tt, :]
        hl_ref[0] = h3[:, tt - 1:tt, :]

    out = x + _mm(gate * h, wout_ref[0])
    if slabs:
        tm_ref[...] = out.reshape(tt, bb, D_MODEL)
        for b in range(bb):
            o_ref[b] = tm_ref[:, b, :]
    else:
        o_ref[...] = out.reshape(o_ref.shape)


def _rg_block(x, group, conv_buf, h0, w, layer, bb, tt):
    nb, t = group.nb, group.t
    j = layer // 2
    slabs = bb == SUBLANES and tt > SUBLANES
    tile = group.tile(bb, tt)
    buf_in = pl.BlockSpec((1, bb, CONV_W - 1, D_RNN), lambda b, i: (j, b, 0, 0))
    h_in = pl.BlockSpec((1, bb, 1, D_RNN), lambda b, i: (j, b, 0, 0))
    buf_out = pl.BlockSpec((1, bb, CONV_W - 1, D_RNN), lambda b, i: (0, b, 0, 0))
    h_out = pl.BlockSpec((1, bb, 1, D_RNN), lambda b, i: (0, b, 0, 0))
    vec = _layer_spec((1, D_RNN), j)
    return pl.pallas_call(
        functools.partial(_rg_kernel, bb=bb, tt=tt, slabs=slabs),
        grid=group.grid(bb, tt),
        in_specs=[tile, buf_in, h_in, _layer_spec((1, D_MODEL), layer),
                  _layer_spec((D_MODEL, 2 * D_RNN), j), _layer_spec((CONV_W, D_RNN), j), vec,
                  _layer_spec((D_RNN // MXU_DIM, MXU_DIM, 2 * MXU_DIM), j), vec, vec, vec,
                  _layer_spec((D_RNN, D_MODEL), j)],
        out_specs=[tile, buf_out, h_out],
        out_shape=[jax.ShapeDtypeStruct(group.shape, F32),
                   jax.ShapeDtypeStruct((1, nb, CONV_W - 1, D_RNN), F32),
                   jax.ShapeDtypeStruct((1, nb, 1, D_RNN), F32)],
        scratch_shapes=[pltpu.VMEM((bb, SUBLANES, D_RNN), F32),
                        pltpu.VMEM((bb, 1, D_RNN), F32),
                        pltpu.VMEM((tt, bb, D_MODEL) if slabs else (1, SUBLANES, LANES), F32)],
        compiler_params=_params(2),
        name="rg_block",
    )(x, conv_buf, h0, w["norm_mix_g"], w["rg_w_in"], w["rg_conv_w"], w["rg_conv_b"],
      w["rg_w_gates"], w["rg_b_a"], w["rg_b_x"], w["rg_lambda"], w["rg_w_out"])


def _block_diag_tiles(w):
    per = MXU_DIM // RG_BW
    w4 = w.reshape(RG_BLOCKS // per, per, RG_BW, RG_BW)
    eye = jnp.eye(per, dtype=w.dtype)
    t = w4[:, :, :, None, :] * eye[None, :, None, :, None]
    return t.reshape(RG_BLOCKS // per, MXU_DIM, MXU_DIM)


_GLA_MAIN = 2 * GLA_DK + 2 * GLA_DV


def _gla_kernel(*refs, bb, tt, chunk, slot, aliased):
    (x_ref, s0_ref, g_ref, wmain_ref, wlo_ref, wa2_ref, ba_ref, ng_ref, wout_ref) = refs[:9]
    o_ref, s_ref, oc_ref = refs[10:] if aliased else refs[9:]
    if aliased:
        slot = 0
    n = bb * tt
    n_seg = n // chunk
    assert bb == 1 or tt == chunk
    assert chunk & (chunk - 1) == 0
    shared_state = bb == 1
    i = pl.program_id(1)

    @pl.when(i == 0)
    def _():
        for other in range(s_ref.shape[0]):
            if other != slot:
                s_ref[other] = jnp.zeros(s_ref.shape[1:], F32)
        s_ref[slot] = s0_ref[0]

    x = x_ref[...].reshape(n, D_MODEL)
    xn = _rmsnorm(x, g_ref[0]).astype(BF16)
    proj = jnp.dot(xn, wmain_ref[0], preferred_element_type=F32)
    q3 = (proj[:, :GLA_DK] * (GLA_HK ** -0.5)).reshape(n_seg, chunk, GLA_DK)
    k3 = proj[:, GLA_DK:2 * GLA_DK].reshape(n_seg, chunk, GLA_DK)
    v = proj[:, 2 * GLA_DK:2 * GLA_DK + GLA_DV]
    gsilu = jax.nn.silu(proj[:, 2 * GLA_DK + GLA_DV:])
    a_lo = jnp.dot(xn, wlo_ref[0], preferred_element_type=F32)
    z = _mm(a_lo, wa2_ref[0]) + ba_ref[0]
    nz = -z
    softplus_nz = jnp.maximum(nz, 0.0) + jnp.log(1.0 + jnp.exp(-jnp.abs(nz)))
    la3 = (softplus_nz * (-LOG2_E / GLA_TAU)).reshape(n_seg, chunk, GLA_DK)

    early_cast = (lambda t: t.astype(BF16)) if chunk % (2 * SUBLANES) == 0 else (lambda t: t)
    v = early_cast(v)
    t_idx = lax.broadcasted_iota(jnp.int32, (n_seg, chunk, GLA_HK), 1)
    rg = min(n, MXU_DIM)
    row = lax.broadcasted_iota(jnp.int32, (rg, rg), 0)
    col = lax.broadcasted_iota(jnp.int32, (rg, rg), 1)
    same_chunk_causal = (col <= row) & (col >= (row & (-chunk)))

    for h in range(GLA_HEADS):
        kc = slice(h * GLA_HK, (h + 1) * GLA_HK)
        vc = slice(h * GLA_HV, (h + 1) * GLA_HV)
        bcum = la3[:, :, kc]
        s = 1
        while s < chunk:
            bcum = bcum + jnp.where(t_idx >= s, pltpu.roll(bcum, s, 1), 0.0)
            s *= 2
        gl = bcum[:, chunk - 1:chunk, :]
        q_in = early_cast((q3[:, :, kc] * jnp.exp2(bcum)).reshape(n, GLA_HK))
        k_dec = k3[:, :, kc] * jnp.exp2(-bcum)
        k_in = early_cast(k_dec.reshape(n, GLA_HK))
        k_end = early_cast((k_dec * jnp.exp2(gl)).reshape(n, GLA_HK))

        for r in range(n // rg):
            rows = slice(r * rg, (r + 1) * rg)
            att = lax.dot_general(q_in[rows].astype(BF16), k_in[rows].astype(BF16),
                                  (((1,), (1,)), ((), ())), preferred_element_type=F32)
            att = jnp.where(same_chunk_causal, att, 0.0).astype(BF16)
            oc_ref[rows, vc] = jnp.dot(att, v[rows, vc].astype(BF16), preferred_element_type=F32)

        state = None
        for c in range(n_seg):
            rows = slice(c * chunk, (c + 1) * chunk)
            sb = 0 if shared_state else c
            if state is None or not shared_state:
                state = s_ref[slot, sb, h]
            oc_ref[rows, vc] = oc_ref[rows, vc] + jnp.dot(
                q_in[rows].astype(BF16), state.astype(BF16), preferred_element_type=F32)
            eg = jnp.exp2(jnp.broadcast_to(gl[c], (GLA_HK, GLA_HK))).T
            eg = jnp.concatenate([eg] * (GLA_HV // GLA_HK), axis=1)
            state = eg * state + lax.dot_general(
                k_end[rows].astype(BF16), v[rows, vc].astype(BF16),
                (((0,), (0,)), ((), ())), preferred_element_type=F32)
            if not shared_state or c == n_seg - 1:
                s_ref[slot, sb, h] = state

    o = oc_ref[...]
    heads = []
    for h in range(GLA_HEADS):
        heads.append(_rmsnorm(o[:, h * GLA_HV:(h + 1) * GLA_HV], ng_ref[0]))
    on = jnp.concatenate(heads, axis=1)
    out = x + _mm(on * gsilu, wout_ref[0])
    o_ref[...] = out.reshape(o_ref.shape)


def _gla_block(x, group, s0, s_new, w, layer, bb, tt, chunk):
    n = bb * tt
    j = layer // 2
    tile = group.tile(bb, tt)
    state = pl.BlockSpec((1, bb, GLA_HEADS, GLA_HK, GLA_HV), lambda b, i: (j, b, 0, 0, 0))
    in_specs = [tile, state, _layer_spec((1, D_MODEL), layer), _layer_spec((D_MODEL, _GLA_MAIN), j),
                _layer_spec((D_MODEL, LANES), j), _layer_spec((LANES, GLA_DK), j),
                _layer_spec((1, GLA_DK), j), _layer_spec((1, GLA_HV), j),
                _layer_spec((GLA_DV, D_MODEL), j)]
    args = [x, s0, w["norm_mix_g"], w["gla_w_main"], w["gla_w_lo"], w["gla_w_a2"], w["gla_b_a"],
            w["gla_norm_g"], w["gla_w_out"]]
    if s_new is None:
        aliases = {}
        state_out = pl.BlockSpec((s0.shape[0], bb, GLA_HEADS, GLA_HK, GLA_HV),
                                 lambda b, i: (0, b, 0, 0, 0))
    else:
        in_specs.append(pl.BlockSpec(memory_space=pl.ANY))
        args.append(s_new)
        aliases = {len(args) - 1: 1}
        state_out = state
    return pl.pallas_call(
        functools.partial(_gla_kernel, bb=bb, tt=tt, chunk=chunk, slot=j,
                          aliased=s_new is not None),
        grid=group.grid(bb, tt),
        in_specs=in_specs,
        out_specs=[tile, state_out],
        out_shape=[jax.ShapeDtypeStruct(group.shape, F32), jax.ShapeDtypeStruct(s0.shape, F32)],
        input_output_aliases=aliases,
        scratch_shapes=[pltpu.VMEM((n, GLA_DV), F32)],
        compiler_params=_params(2),
        name="gla_block",
    )(*args)


class _Stream(NamedTuple):
    x: jax.Array
    group: _Group
    mem_k: jax.Array
    mem_v: jax.Array
    rg_h: jax.Array
    rg_conv: jax.Array
    gla_s: jax.Array
    tiles: dict


def _run_layers(streams, w):
    xs = [s.x for s in streams]
    hs = [[] for _ in streams]
    convs = [[] for _ in streams]
    s_new = [None for _ in streams]
    for layer in range(DEPTH):
        for n, s in enumerate(streams):
            if layer % 2 == 0:
                xs[n], cb, hl = _rg_block(xs[n], s.group, s.rg_conv, s.rg_h, w, layer,
                                          *s.tiles["rg"])
                convs[n].append(cb)
                hs[n].append(hl)
            else:
                xs[n], s_new[n] = _gla_block(xs[n], s.group, s.gla_s, s_new[n], w, layer,
                                             *s.tiles["gla"])
            xs[n] = _xattn(xs[n], s.group, w, layer, s.mem_k, s.mem_v, *s.tiles["xattn"])
        a, b = streams
        xs = list(_mlp(xs[0], a.group, a.tiles["mlp"], xs[1], b.group, b.tiles["mlp"], w, layer))
    results = []
    for n, s in enumerate(streams):
        h_all = jnp.concatenate(hs[n], axis=0).reshape(len(hs[n]), s.group.nb, D_RNN)
        results.append((xs[n], h_all, jnp.concatenate(convs[n], axis=0), s_new[n]))
    return results


def kernel(x_prompt, x_sample, mem_prompt, state_rglru_h, state_rglru_conv, state_gla_S, cache_mem_k, cache_mem_v, norm_mix_g, norm_xa_g, norm_mem_g, norm_mlp_g, final_norm_g, rg_w_in, rg_conv_w, rg_conv_b, rg_w_a, rg_b_a, rg_w_x, rg_b_x, rg_lambda, rg_w_out, gla_w_in, gla_w_a2, gla_b_a, gla_norm_g, gla_w_out, xa_wq, xa_wk, xa_wv, xa_wo, mlp_w1, mlp_w2):
    batch, seq, _ = x_prompt.shape
    dec_batch, dec_seq, _ = x_sample.shape
    n_a = rg_w_in.shape[0]
    n_b = gla_w_in.shape[0]
    assert batch == SUBLANES and dec_seq == SUBLANES

    def rows(p):
        return p.reshape(p.shape[0], 1, p.shape[1])

    gates = jnp.concatenate(
        [jax.vmap(_block_diag_tiles)(rg_w_a), jax.vmap(_block_diag_tiles)(rg_w_x)], axis=-1)
    lo_pad = LANES - GLA_RANK
    w = dict(
        norm_mix_g=rows(norm_mix_g), norm_xa_g=rows(norm_xa_g), norm_mlp_g=rows(norm_mlp_g),
        final_norm_g=final_norm_g.reshape(1, D_MODEL),
        rg_w_in=rg_w_in.astype(BF16), rg_conv_w=rg_conv_w, rg_conv_b=rows(rg_conv_b),
        rg_w_gates=gates.astype(BF16), rg_b_a=rows(rg_b_a), rg_b_x=rows(rg_b_x),
        rg_lambda=rows(rg_lambda), rg_w_out=rg_w_out.astype(BF16),
        gla_w_main=gla_w_in[:, :, :_GLA_MAIN].astype(BF16),
        gla_w_lo=jnp.pad(gla_w_in[:, :, _GLA_MAIN:], ((0, 0), (0, 0), (0, lo_pad))).astype(BF16),
        gla_w_a2=jnp.pad(gla_w_a2, ((0, 0), (0, lo_pad), (0, 0))).astype(BF16),
        gla_b_a=rows(gla_b_a), gla_norm_g=rows(gla_norm_g), gla_w_out=gla_w_out.astype(BF16),
        xa_wq=xa_wq.astype(BF16), xa_wo=xa_wo.astype(BF16),
        mlp_w1=mlp_w1.astype(BF16), mlp_w2=mlp_w2.astype(BF16),
    )

    mem2d = mem_prompt.reshape(batch * N_MEM, D_MODEL)
    mem_k_prompt, mem_v_prompt, mkb, mvb = _mem_kv(
        mem2d, rows(norm_mem_g), xa_wk.astype(BF16), xa_wv.astype(BF16))
    prompt_tiles = dict(rg=(SUBLANES, ROW_TILE // SUBLANES), gla=(1, ROW_TILE, GLA_CHUNK),
                        xattn=(1, ROW_TILE), mlp=(1, ROW_TILE))
    prompt = _Stream(
        x_prompt, _Group(batch, seq),
        mkb.reshape(DEPTH, batch, N_MEM, D_MODEL), mvb.reshape(DEPTH, batch, N_MEM, D_MODEL),
        jnp.zeros((n_a, batch, 1, D_RNN), F32), jnp.zeros((n_a, batch, CONV_W - 1, D_RNN), F32),
        jnp.zeros((n_b, batch, GLA_HEADS, GLA_HK, GLA_HV), F32), prompt_tiles)

    sample_chunk = GLA_CHUNK if dec_seq % GLA_CHUNK == 0 else dec_seq
    sample_tiles = dict(rg=(32, dec_seq), gla=(8, dec_seq, sample_chunk), xattn=(8, dec_seq),
                        mlp=(ROW_TILE // dec_seq, dec_seq))
    sample = _Stream(
        x_sample, _Group(dec_batch, dec_seq), cache_mem_k, cache_mem_v,
        state_rglru_h.reshape(n_a, dec_batch, 1, D_RNN), state_rglru_conv, state_gla_S,
        sample_tiles)

    (y_p, h_p, conv_p, s_p), (y_s, h_s, conv_s, s_s) = _run_layers((prompt, sample), w)
    return (y_p, y_s, mem_k_prompt, mem_v_prompt, h_p, conv_p, s_p, h_s, conv_s, s_s)
```

```python
import functools
from typing import NamedTuple

import jax
import jax.numpy as jnp
from jax import lax
from jax.experimental import pallas as pl
from jax.experimental.pallas import tpu as pltpu

F32 = jnp.float32
BF16 = jnp.bfloat16

D_MODEL = 1024
DEPTH = 4
D_RNN = D_MODEL
RG_BLOCKS = 16
RG_BW = D_RNN // RG_BLOCKS
CONV_W = 4
RG_C = 8.0
GLA_HEADS = 4
GLA_DK = D_MODEL // 2
GLA_DV = D_MODEL
GLA_HK = GLA_DK // GLA_HEADS
GLA_HV = GLA_DV // GLA_HEADS
GLA_RANK = 16
GLA_TAU = 16.0
GLA_CHUNK = 64
N_MEM = 256
XA_HEADS = 4
XA_HD = D_MODEL // XA_HEADS
D_FF = 4 * D_MODEL
EPS = 1e-6

SUBLANES = 8
LANES = 128
MXU_DIM = 256
VMEM_LIMIT_BYTES = 56 * 1024 * 1024
MASKED_SCORE = -1e30
LOG2_E = 1.4426950408889634
ROW_TILE = 512


class _Group(NamedTuple):
    nb: int
    t: int

    @property
    def shape(self):
        return (self.nb, self.t, D_MODEL)

    def tile(self, bb, tt):
        return pl.BlockSpec((bb, tt, D_MODEL), lambda b, i: (b, i, 0))

    def grid(self, bb, tt):
        return (self.nb // bb, self.t // tt)


def _params(n_axes):
    return pltpu.CompilerParams(
        dimension_semantics=("arbitrary",) * n_axes,
        vmem_limit_bytes=VMEM_LIMIT_BYTES,
    )


def _layer_spec(shape, layer):
    zeros = (0,) * len(shape)
    return pl.BlockSpec((1,) + tuple(shape), lambda *_: (layer,) + zeros,
                        pipeline_mode=pl.Buffered(1))


def _rmsnorm(x, g):
    ms = jnp.mean(x * x, axis=-1, keepdims=True)
    return x * lax.rsqrt(ms + EPS) * g


def _mm(a, w):
    return jnp.dot(a.astype(BF16), w, preferred_element_type=F32)


def _softplus(z):
    return jnp.maximum(z, 0.0) + jnp.log1p(jnp.exp(-jnp.abs(z)))


def _memkv_kernel(mem_ref, g_ref, wk_ref, wv_ref, k_ref, v_ref, kb_ref, vb_ref, *, nbm):
    mn = _rmsnorm(mem_ref[...], g_ref[0]).astype(BF16)
    k = jnp.dot(mn, wk_ref[0], preferred_element_type=F32)
    v = jnp.dot(mn, wv_ref[0], preferred_element_type=F32)
    k_ref[0] = k.reshape(nbm, N_MEM, XA_HEADS, XA_HD)
    v_ref[0] = v.reshape(nbm, N_MEM, XA_HEADS, XA_HD)
    kb_ref[0] = k.astype(BF16)
    vb_ref[0] = v.astype(BF16)


def _mem_kv(mem2d, norm_g, wk, wv, nbm=2):
    n = mem2d.shape[0]
    nb = n // N_MEM
    tm = nbm * N_MEM
    row = pl.BlockSpec((tm, D_MODEL), lambda l, i: (i, 0))
    per_layer = lambda shape: pl.BlockSpec((1,) + shape, lambda l, i: (l, 0, 0))
    out5 = pl.BlockSpec((1, nbm, N_MEM, XA_HEADS, XA_HD), lambda l, i: (l, i, 0, 0, 0))
    out3 = pl.BlockSpec((1, tm, D_MODEL), lambda l, i: (l, i, 0))
    return pl.pallas_call(
        functools.partial(_memkv_kernel, nbm=nbm),
        grid=(DEPTH, n // tm),
        in_specs=[row, per_layer((1, D_MODEL)), per_layer((D_MODEL, D_MODEL)),
                  per_layer((D_MODEL, D_MODEL))],
        out_specs=[out5, out5, out3, out3],
        out_shape=[jax.ShapeDtypeStruct((DEPTH, nb, N_MEM, XA_HEADS, XA_HD), F32)] * 2
        + [jax.ShapeDtypeStruct((DEPTH, n, D_MODEL), BF16)] * 2,
        compiler_params=_params(2),
        name="mem_kv",
    )(mem2d, norm_g, wk, wv)


def _mlp_kernel(xa_ref, xb_ref, g_ref, w1_ref, w2_ref, gf_ref, oa_ref, ob_ref, *,
                steps_a, final, f_chunk):
    def apply(x_ref, o_ref):
        x = x_ref[...].reshape(-1, D_MODEL)
        xn = _rmsnorm(x, g_ref[0]).astype(BF16)
        acc = x
        for c in range(D_FF // f_chunk):
            cols = slice(c * f_chunk, (c + 1) * f_chunk)
            hid = jnp.dot(xn, w1_ref[0, :, cols], preferred_element_type=F32)
            act = jnp.square(jnp.maximum(hid, 0.0)).astype(BF16)
            acc = acc + jnp.dot(act, w2_ref[0, cols, :], preferred_element_type=F32)
        if final:
            acc = _rmsnorm(acc, gf_ref[...])
        o_ref[...] = acc.reshape(o_ref.shape)

    p = pl.program_id(0)
    pl.when(p < steps_a)(lambda: apply(xa_ref, oa_ref))
    pl.when(p >= steps_a)(lambda: apply(xb_ref, ob_ref))


def _mlp(xa, group_a, tile_a, xb, group_b, tile_b, w, layer, f_chunk=1024):
    final = layer == DEPTH - 1
    t_steps_a = group_a.grid(*tile_a)[1]
    t_steps_b = group_b.grid(*tile_b)[1]
    steps_a = group_a.grid(*tile_a)[0] * t_steps_a
    steps_b = group_b.grid(*tile_b)[0] * t_steps_b

    def index_a(p):
        q = jnp.minimum(p, steps_a - 1)
        return (q // t_steps_a, q % t_steps_a, 0)

    def index_b(p):
        q = jnp.maximum(p - steps_a, 0)
        return (q // t_steps_b, q % t_steps_b, 0)

    spec_a = pl.BlockSpec(tuple(tile_a) + (D_MODEL,), index_a)
    spec_b = pl.BlockSpec(tuple(tile_b) + (D_MODEL,), index_b)
    return pl.pallas_call(
        functools.partial(_mlp_kernel, steps_a=steps_a, final=final, f_chunk=f_chunk),
        grid=(steps_a + steps_b,),
        in_specs=[spec_a, spec_b, _layer_spec((1, D_MODEL), layer),
                  _layer_spec((D_MODEL, D_FF), layer), _layer_spec((D_FF, D_MODEL), layer),
                  pl.BlockSpec((1, D_MODEL), lambda p: (0, 0))],
        out_specs=[spec_a, spec_b],
        out_shape=[jax.ShapeDtypeStruct(group_a.shape, F32),
                   jax.ShapeDtypeStruct(group_b.shape, F32)],
        compiler_params=_params(1),
        name="mlp",
    )(xa, xb, w["norm_mlp_g"], w["mlp_w1"], w["mlp_w2"], w["final_norm_g"])


def _xattn_kernel(x_ref, g_ref, wq_ref, wo_ref, k_ref, v_ref, o_ref, att_ref, *, bb, tt, packed_heads):
    n = bb * tt
    x = x_ref[...].reshape(n, D_MODEL)
    xn = _rmsnorm(x, g_ref[0])
    q = _mm(xn, wq_ref[0]) * (XA_HD ** -0.5 * LOG2_E)

    def softmax(s):
        e = jnp.exp2(s - jnp.max(s, axis=-1, keepdims=True))
        return (e / jnp.sum(e, axis=-1, keepdims=True)).astype(BF16)

    nt_dims = (((1,), (1,)), ((), ()))
    if packed_heads:
        rows_b = XA_HEADS * tt
        shape = (bb * rows_b, N_MEM * XA_HEADS)
        q_head = (lax.broadcasted_iota(jnp.int32, shape, 0) % rows_b) // tt
        kv_head = lax.broadcasted_iota(jnp.int32, shape, 1) % XA_HEADS
        same_head = q_head == kv_head
        scores = []
        for b in range(bb):
            qs = jnp.concatenate(
                [q[b * tt:(b + 1) * tt, h * XA_HD:(h + 1) * XA_HD] for h in range(XA_HEADS)],
                axis=0).astype(BF16)
            kf = k_ref[0, b].reshape(N_MEM * XA_HEADS, XA_HD).astype(BF16)
            scores.append(lax.dot_general(qs, kf, nt_dims, preferred_element_type=F32))
        s = jnp.concatenate(scores, axis=0)
        p = softmax(jnp.where(same_head, s, MASKED_SCORE))
        for b in range(bb):
            vf = v_ref[0, b].reshape(N_MEM * XA_HEADS, XA_HD).astype(BF16)
            o = jnp.dot(p[b * rows_b:(b + 1) * rows_b], vf, preferred_element_type=F32)
            for h in range(XA_HEADS):
                att_ref[b * tt:(b + 1) * tt, h * XA_HD:(h + 1) * XA_HD] = o[h * tt:(h + 1) * tt]
    else:
        qb = q.astype(BF16)
        for b in range(bb):
            rows = slice(b * tt, (b + 1) * tt)
            head_cols = [slice(h * XA_HD, (h + 1) * XA_HD) for h in range(XA_HEADS)]
            s = jnp.concatenate(
                [lax.dot_general(qb[rows, cols], k_ref[0, b, :, cols], nt_dims,
                                 preferred_element_type=F32) for cols in head_cols], axis=0)
            p = softmax(s)
            for h, cols in enumerate(head_cols):
                att_ref[rows, cols] = jnp.dot(
                    p[h * tt:(h + 1) * tt], v_ref[0, b, :, cols], preferred_element_type=F32)
    out = x + _mm(att_ref[...], wo_ref[0])
    o_ref[...] = out.reshape(o_ref.shape)


def _xattn(x, group, w, layer, k, v, bb, tt):
    packed_heads = k.ndim == 5
    if packed_heads:
        kv = pl.BlockSpec((1, bb, N_MEM, XA_HEADS, XA_HD), lambda b, i: (layer, b, 0, 0, 0))
    else:
        kv = pl.BlockSpec((1, bb, N_MEM, D_MODEL), lambda b, i: (layer, b, 0, 0))
    return pl.pallas_call(
        functools.partial(_xattn_kernel, bb=bb, tt=tt, packed_heads=packed_heads),
        grid=group.grid(bb, tt),
        in_specs=[group.tile(bb, tt), _layer_spec((1, D_MODEL), layer),
                  _layer_spec((D_MODEL, D_MODEL), layer), _layer_spec((D_MODEL, D_MODEL), layer),
                  kv, kv],
        out_specs=group.tile(bb, tt),
        out_shape=jax.ShapeDtypeStruct(group.shape, F32),
        scratch_shapes=[pltpu.VMEM((bb * tt, D_MODEL), F32)],
        compiler_params=_params(2),
        name="xattn",
    )(x, w["norm_xa_g"], w["xa_wq"], w["xa_wo"], k, v)


def _rg_kernel(x_ref, buf_ref, h0_ref, g_ref, win_ref, cw_ref, cb_ref, wg_ref, ba_ref, bx_ref,
               lam_ref, wout_ref, o_ref, nbuf_ref, hl_ref, u_ref, hc_ref, tm_ref, *, bb, tt, slabs):
    n = bb * tt
    pad = SUBLANES
    n_prev = CONV_W - 1
    assert (bb == SUBLANES) if slabs else (tt == SUBLANES)
    i = pl.program_id(1)

    @pl.when(i == 0)
    def _():
        u_ref[...] = jnp.zeros((bb, pad, D_RNN), F32)
        if slabs:
            for j in range(n_prev):
                u_ref[pad - n_prev + j] = buf_ref[0, :, j, :]
        else:
            u_ref[:, pad - n_prev:pad, :] = buf_ref[0]
        hc_ref[...] = h0_ref[0]

    if slabs:
        x = jnp.concatenate([x_ref[:, t, :] for t in range(tt)], axis=0)
    else:
        x = x_ref[...].reshape(n, D_MODEL)
    xn = _rmsnorm(x, g_ref[0])
    yx = _mm(xn, win_ref[0])
    gate = jax.nn.gelu(yx[:, :D_RNN])
    u = yx[:, D_RNN:]

    def tap(j):
        return cw_ref[0, CONV_W - 1 - j:CONV_W - j, :]

    if slabs:
        steps = [u_ref[pad - n_prev + j] for j in range(n_prev)]
        steps += [u[t * bb:(t + 1) * bb] for t in range(tt)]
        xc_steps = []
        for t in range(tt):
            acc = cb_ref[0] + tap(0) * steps[n_prev + t]
            for j in range(1, CONV_W):
                acc = acc + tap(j) * steps[n_prev + t - j]
            xc_steps.append(acc)
        xc2 = jnp.concatenate(xc_steps, axis=0)
        for j in range(n_prev):
            u_ref[pad - n_prev + j] = steps[tt + j]
            nbuf_ref[0, :, j, :] = steps[tt + j]
    else:
        u3 = u.reshape(bb, tt, D_RNN)
        ext = jnp.concatenate([u_ref[...], u3], axis=1)
        xc = cb_ref[0].reshape(1, 1, D_RNN) + tap(0).reshape(1, 1, D_RNN) * u3
        for j in range(1, CONV_W):
            xc = xc + tap(j).reshape(1, 1, D_RNN) * pltpu.roll(ext, j, 1)[:, pad:, :]
        nbuf_ref[0] = ext[:, pad + tt - n_prev:, :]
        u_ref[...] = ext[:, tt:, :]
        xc2 = xc.reshape(n, D_RNN)

    gates = [_mm(xc2[:, c * MXU_DIM:(c + 1) * MXU_DIM], wg_ref[0, c]) for c in range(D_RNN // MXU_DIM)]
    r = jax.nn.sigmoid(jnp.concatenate([gc[:, :MXU_DIM] for gc in gates], axis=1) + ba_ref[0])
    ig = jax.nn.sigmoid(jnp.concatenate([gc[:, MXU_DIM:] for gc in gates], axis=1) + bx_ref[0])
    log_a = r * ((-RG_C) * _softplus(-lam_ref[0]))
    a = jnp.exp(log_a)
    m2 = -jnp.tanh(log_a) * (a * a + 1.0)
    mult = jnp.where(m2 > 0.0, m2 * lax.rsqrt(m2), 0.0)
    bt = mult * (ig * xc2)

    if slabs:
        carry = hc_ref[:, 0, :]
        h_steps = []
        for t in range(tt):
            rows = slice(t * bb, (t + 1) * bb)
            carry = a[rows] * carry + bt[rows]
            h_steps.append(carry)
        h = jnp.concatenate(h_steps, axis=0)
        hc_ref[:, 0, :] = carry
        hl_ref[0, :, 0, :] = carry
    else:
        ag = a.reshape(bb, tt, D_RNN)
        bg = bt.reshape(bb, tt, D_RNN)
        t_idx = lax.broadcasted_iota(jnp.int32, (bb, tt, D_RNN), 1)
        s = 1
        while s < tt:
            keep = t_idx >= s
            a_s = jnp.where(keep, pltpu.roll(ag, s, 1), 1.0)
            b_s = jnp.where(keep, pltpu.roll(bg, s, 1), 0.0)
            bg = ag * b_s + bg
            ag = ag * a_s
            s *= 2
        h3 = ag * hc_ref[...] + bg
        h = h3.reshape(n, D_RNN)
        hc_ref[...] = h3[:, tt - 1:tt, :]
        hl_ref[0] = h3[:, tt - 1:tt, :]

    out = x + _mm(gate * h, wout_ref[0])
    if slabs:
        tm_ref[...] = out.reshape(tt, bb, D_MODEL)
        for b in range(bb):
            o_ref[b] = tm_ref[:, b, :]
    else:
        o_ref[...] = out.reshape(o_ref.shape)


def _rg_block(x, group, conv_buf, h0, w, layer, bb, tt):
    nb, t = group.nb, group.t
    j = layer // 2
    slabs = bb == SUBLANES and tt > SUBLANES
    tile = group.tile(bb, tt)
    buf_in = pl.BlockSpec((1, bb, CONV_W - 1, D_RNN), lambda b, i: (j, b, 0, 0))
    h_in = pl.BlockSpec((1, bb, 1, D_RNN), lambda b, i: (j, b, 0, 0))
    buf_out = pl.BlockSpec((1, bb, CONV_W - 1, D_RNN), lambda b, i: (0, b, 0, 0))
    h_out = pl.BlockSpec((1, bb, 1, D_RNN), lambda b, i: (0, b, 0, 0))
    vec = _layer_spec((1, D_RNN), j)
    return pl.pallas_call(
        functools.partial(_rg_kernel, bb=bb, tt=tt, slabs=slabs),
        grid=group.grid(bb, tt),
        in_specs=[tile, buf_in, h_in, _layer_spec((1, D_MODEL), layer),
                  _layer_spec((D_MODEL, 2 * D_RNN), j), _layer_spec((CONV_W, D_RNN), j), vec,
                  _layer_spec((D_RNN // MXU_DIM, MXU_DIM, 2 * MXU_DIM), j), vec, vec, vec,
                  _layer_spec((D_RNN, D_MODEL), j)],
        out_specs=[tile, buf_out, h_out],
        out_shape=[jax.ShapeDtypeStruct(group.shape, F32),
                   jax.ShapeDtypeStruct((1, nb, CONV_W - 1, D_RNN), F32),
                   jax.ShapeDtypeStruct((1, nb, 1, D_RNN), F32)],
        scratch_shapes=[pltpu.VMEM((bb, SUBLANES, D_RNN), F32),
                        pltpu.VMEM((bb, 1, D_RNN), F32),
                        pltpu.VMEM((tt, bb, D_MODEL) if slabs else (1, SUBLANES, LANES), F32)],
        compiler_params=_params(2),
        name="rg_block",
    )(x, conv_buf, h0, w["norm_mix_g"], w["rg_w_in"], w["rg_conv_w"], w["rg_conv_b"],
      w["rg_w_gates"], w["rg_b_a"], w["rg_b_x"], w["rg_lambda"], w["rg_w_out"])


def _block_diag_tiles(w):
    per = MXU_DIM // RG_BW
    w4 = w.reshape(RG_BLOCKS // per, per, RG_BW, RG_BW)
    eye = jnp.eye(per, dtype=w.dtype)
    t = w4[:, :, :, None, :] * eye[None, :, None, :, None]
    return t.reshape(RG_BLOCKS // per, MXU_DIM, MXU_DIM)


_GLA_MAIN = 2 * GLA_DK + 2 * GLA_DV


def _gla_kernel(*refs, bb, tt, chunk, slot, aliased):
    (x_ref, s0_ref, g_ref, wmain_ref, wlo_ref, wa2_ref, ba_ref, ng_ref, wout_ref) = refs[:9]
    o_ref, s_ref, oc_ref = refs[10:] if aliased else refs[9:]
    if aliased:
        slot = 0
    n = bb * tt
    n_seg = n // chunk
    assert bb == 1 or tt == chunk
    assert chunk & (chunk - 1) == 0
    shared_state = bb == 1
    i = pl.program_id(1)

    @pl.when(i == 0)
    def _():
        for other in range(s_ref.shape[0]):
            if other != slot:
                s_ref[other] = jnp.zeros(s_ref.shape[1:], F32)
        s_ref[slot] = s0_ref[0]

    x = x_ref[...].reshape(n, D_MODEL)
    xn = _rmsnorm(x, g_ref[0]).astype(BF16)
    proj = jnp.dot(xn, wmain_ref[0], preferred_element_type=F32)
    q3 = (proj[:, :GLA_DK] * (GLA_HK ** -0.5)).reshape(n_seg, chunk, GLA_DK)
    k3 = proj[:, GLA_DK:2 * GLA_DK].reshape(n_seg, chunk, GLA_DK)
    v = proj[:, 2 * GLA_DK:2 * GLA_DK + GLA_DV]
    gsilu = jax.nn.silu(proj[:, 2 * GLA_DK + GLA_DV:])
    a_lo = jnp.dot(xn, wlo_ref[0], preferred_element_type=F32)
    z = _mm(a_lo, wa2_ref[0]) + ba_ref[0]
    nz = -z
    softplus_nz = jnp.maximum(nz, 0.0) + jnp.log(1.0 + jnp.exp(-jnp.abs(nz)))
    la3 = (softplus_nz * (-LOG2_E / GLA_TAU)).reshape(n_seg, chunk, GLA_DK)

    early_cast = (lambda t: t.astype(BF16)) if chunk % (2 * SUBLANES) == 0 else (lambda t: t)
    v = early_cast(v)
    t_idx = lax.broadcasted_iota(jnp.int32, (n_seg, chunk, GLA_HK), 1)
    rg = min(n, MXU_DIM)
    row = lax.broadcasted_iota(jnp.int32, (rg, rg), 0)
    col = lax.broadcasted_iota(jnp.int32, (rg, rg), 1)
    same_chunk_causal = (col <= row) & (col >= (row & (-chunk)))

    for h in range(GLA_HEADS):
        kc = slice(h * GLA_HK, (h + 1) * GLA_HK)
        vc = slice(h * GLA_HV, (h + 1) * GLA_HV)
        bcum = la3[:, :, kc]
        s = 1
        while s < chunk:
            bcum = bcum + jnp.where(t_idx >= s, pltpu.roll(bcum, s, 1), 0.0)
            s *= 2
        gl = bcum[:, chunk - 1:chunk, :]
        q_in = early_cast((q3[:, :, kc] * jnp.exp2(bcum)).reshape(n, GLA_HK))
        k_dec = k3[:, :, kc] * jnp.exp2(-bcum)
        k_in = early_cast(k_dec.reshape(n, GLA_HK))
        k_end = early_cast((k_dec * jnp.exp2(gl)).reshape(n, GLA_HK))

        for r in range(n // rg):
            rows = slice(r * rg, (r + 1) * rg)
            att = lax.dot_general(q_in[rows].astype(BF16), k_in[rows].astype(BF16),
                                  (((1,), (1,)), ((), ())), preferred_element_type=F32)
            att = jnp.where(same_chunk_causal, att, 0.0).astype(BF16)
            oc_ref[rows, vc] = jnp.dot(att, v[rows, vc].astype(BF16), preferred_element_type=F32)

        state = None
        for c in range(n_seg):
            rows = slice(c * chunk, (c + 1) * chunk)
            sb = 0 if shared_state else c
            if state is None or not shared_state:
                state = s_ref[slot, sb, h]
            oc_ref[rows, vc] = oc_ref[rows, vc] + jnp.dot(
                q_in[rows].astype(BF16), state.astype(BF16), preferred_element_type=F32)
            eg = jnp.exp2(jnp.broadcast_to(gl[c], (GLA_HK, GLA_HK))).T
            eg = jnp.concatenate([eg] * (GLA_HV // GLA_HK), axis=1)
            state = eg * state + lax.dot_general(
                k_end[rows].astype(BF16), v[rows, vc].astype(BF16),
                (((0,), (0,)), ((), ())), preferred_element_type=F32)
            if not shared_state or c == n_seg - 1:
                s_ref[slot, sb, h] = state

    o = oc_ref[...]
    heads = []
    for h in range(GLA_HEADS):
        heads.append(_rmsnorm(o[:, h * GLA_HV:(h + 1) * GLA_HV], ng_ref[0]))
    on = jnp.concatenate(heads, axis=1)
    out = x + _mm(on * gsilu, wout_ref[0])
    o_ref[...] = out.reshape(o_ref.shape)


def _gla_block(x, group, s0, s_new, w, layer, bb, tt, chunk):
    n = bb * tt
    j = layer // 2
    tile = group.tile(bb, tt)
    state = pl.BlockSpec((1, bb, GLA_HEADS, GLA_HK, GLA_HV), lambda b, i: (j, b, 0, 0, 0))
    in_specs = [tile, state, _layer_spec((1, D_MODEL), layer), _layer_spec((D_MODEL, _GLA_MAIN), j),
                _layer_spec((D_MODEL, LANES), j), _layer_spec((LANES, GLA_DK), j),
                _layer_spec((1, GLA_DK), j), _layer_spec((1, GLA_HV), j),
                _layer_spec((GLA_DV, D_MODEL), j)]
    args = [x, s0, w["norm_mix_g"], w["gla_w_main"], w["gla_w_lo"], w["gla_w_a2"], w["gla_b_a"],
            w["gla_norm_g"], w["gla_w_out"]]
    if s_new is None:
        aliases = {}
        state_out = pl.BlockSpec((s0.shape[0], bb, GLA_HEADS, GLA_HK, GLA_HV),
                                 lambda b, i: (0, b, 0, 0, 0))
    else:
        in_specs.append(pl.BlockSpec(memory_space=pl.ANY))
        args.append(s_new)
        aliases = {len(args) - 1: 1}
        state_out = state
    return pl.pallas_call(
        functools.partial(_gla_kernel, bb=bb, tt=tt, chunk=chunk, slot=j,
                          aliased=s_new is not None),
        grid=group.grid(bb, tt),
        in_specs=in_specs,
        out_specs=[tile, state_out],
        out_shape=[jax.ShapeDtypeStruct(group.shape, F32), jax.ShapeDtypeStruct(s0.shape, F32)],
        input_output_aliases=aliases,
        scratch_shapes=[pltpu.VMEM((n, GLA_DV), F32)],
        compiler_params=_params(2),
        name="gla_block",
    )(*args)


class _Stream(NamedTuple):
    x: jax.Array
    group: _Group
    mem_k: jax.Array
    mem_v: jax.Array
    rg_h: jax.Array
    rg_conv: jax.Array
    gla_s: jax.Array
    tiles: dict


def _run_layers(streams, w):
    xs = [s.x for s in streams]
    hs = [[] for _ in streams]
    convs = [[] for _ in streams]
    s_new = [None for _ in streams]
    for layer in range(DEPTH):
        for n, s in enumerate(streams):
            if layer % 2 == 0:
                xs[n], cb, hl = _rg_block(xs[n], s.group, s.rg_conv, s.rg_h, w, layer,
                                          *s.tiles["rg"])
                convs[n].append(cb)
                hs[n].append(hl)
            else:
                gla_tile = s.tiles["gla"] if s_new[n] is None else s.tiles["gla_in_place"]
                xs[n], s_new[n] = _gla_block(xs[n], s.group, s.gla_s, s_new[n], w, layer,
                                             *gla_tile)
            xs[n] = _xattn(xs[n], s.group, w, layer, s.mem_k, s.mem_v, *s.tiles["xattn"])
        a, b = streams
        xs = list(_mlp(xs[0], a.group, a.tiles["mlp"], xs[1], b.group, b.tiles["mlp"], w, layer))
    results = []
    for n, s in enumerate(streams):
        h_all = jnp.concatenate(hs[n], axis=0).reshape(len(hs[n]), s.group.nb, D_RNN)
        results.append((xs[n], h_all, jnp.concatenate(convs[n], axis=0), s_new[n]))
    return results


def kernel(x_prompt, x_sample, mem_prompt, state_rglru_h, state_rglru_conv, state_gla_S, cache_mem_k, cache_mem_v, norm_mix_g, norm_xa_g, norm_mem_g, norm_mlp_g, final_norm_g, rg_w_in, rg_conv_w, rg_conv_b, rg_w_a, rg_b_a, rg_w_x, rg_b_x, rg_lambda, rg_w_out, gla_w_in, gla_w_a2, gla_b_a, gla_norm_g, gla_w_out, xa_wq, xa_wk, xa_wv, xa_wo, mlp_w1, mlp_w2):
    batch, seq, _ = x_prompt.shape
    dec_batch, dec_seq, _ = x_sample.shape
    n_a = rg_w_in.shape[0]
    n_b = gla_w_in.shape[0]
    assert batch == SUBLANES and dec_seq == SUBLANES

    def rows(p):
        return p.reshape(p.shape[0], 1, p.shape[1])

    gates = jnp.concatenate(
        [jax.vmap(_block_diag_tiles)(rg_w_a), jax.vmap(_block_diag_tiles)(rg_w_x)], axis=-1)
    lo_pad = LANES - GLA_RANK
    w = dict(
        norm_mix_g=rows(norm_mix_g), norm_xa_g=rows(norm_xa_g), norm_mlp_g=rows(norm_mlp_g),
        final_norm_g=final_norm_g.reshape(1, D_MODEL),
        rg_w_in=rg_w_in.astype(BF16), rg_conv_w=rg_conv_w, rg_conv_b=rows(rg_conv_b),
        rg_w_gates=gates.astype(BF16), rg_b_a=rows(rg_b_a), rg_b_x=rows(rg_b_x),
        rg_lambda=rows(rg_lambda), rg_w_out=rg_w_out.astype(BF16),
        gla_w_main=gla_w_in[:, :, :_GLA_MAIN].astype(BF16),
        gla_w_lo=jnp.pad(gla_w_in[:, :, _GLA_MAIN:], ((0, 0), (0, 0), (0, lo_pad))).astype(BF16),
        gla_w_a2=jnp.pad(gla_w_a2, ((0, 0), (0, lo_pad), (0, 0))).astype(BF16),
        gla_b_a=rows(gla_b_a), gla_norm_g=rows(gla_norm_g), gla_w_out=gla_w_out.astype(BF16),
        xa_wq=xa_wq.astype(BF16), xa_wo=xa_wo.astype(BF16),
        mlp_w1=mlp_w1.astype(BF16), mlp_w2=mlp_w2.astype(BF16),
    )

    mem2d = mem_prompt.reshape(batch * N_MEM, D_MODEL)
    mem_k_prompt, mem_v_prompt, mkb, mvb = _mem_kv(
        mem2d, rows(norm_mem_g), xa_wk.astype(BF16), xa_wv.astype(BF16))
    prompt_tiles = dict(rg=(SUBLANES, ROW_TILE // SUBLANES), gla=(1, ROW_TILE, GLA_CHUNK),
                        gla_in_place=(1, ROW_TILE, GLA_CHUNK), xattn=(1, ROW_TILE),
                        mlp=(1, ROW_TILE))
    prompt = _Stream(
        x_prompt, _Group(batch, seq),
        mkb.reshape(DEPTH, batch, N_MEM, D_MODEL), mvb.reshape(DEPTH, batch, N_MEM, D_MODEL),
        jnp.zeros((n_a, batch, 1, D_RNN), F32), jnp.zeros((n_a, batch, CONV_W - 1, D_RNN), F32),
        jnp.zeros((n_b, batch, GLA_HEADS, GLA_HK, GLA_HV), F32), prompt_tiles)

    sample_chunk = GLA_CHUNK if dec_seq % GLA_CHUNK == 0 else dec_seq
    sample_tiles = dict(rg=(32, dec_seq), gla=(8, dec_seq, sample_chunk),
                        gla_in_place=(16, dec_seq, sample_chunk), xattn=(8, dec_seq),
                        mlp=(ROW_TILE // dec_seq, dec_seq))
    sample = _Stream(
        x_sample, _Group(dec_batch, dec_seq), cache_mem_k, cache_mem_v,
        state_rglru_h.reshape(n_a, dec_batch, 1, D_RNN), state_rglru_conv, state_gla_S,
        sample_tiles)

    (y_p, h_p, conv_p, s_p), (y_s, h_s, conv_s, s_s) = _run_layers((prompt, sample), w)
    return (y_p, y_s, mem_k_prompt, mem_v_prompt, h_p, conv_p, s_p, h_s, conv_s, s_s)
```

```python
import functools
from typing import NamedTuple

import jax
import jax.numpy as jnp
from jax import lax
from jax.experimental import pallas as pl
from jax.experimental.pallas import tpu as pltpu

F32 = jnp.float32
BF16 = jnp.bfloat16

D_MODEL = 1024
DEPTH = 4
D_RNN = D_MODEL
RG_BLOCKS = 16
RG_BW = D_RNN // RG_BLOCKS
CONV_W = 4
RG_C = 8.0
GLA_HEADS = 4
GLA_DK = D_MODEL // 2
GLA_DV = D_MODEL
GLA_HK = GLA_DK // GLA_HEADS
GLA_HV = GLA_DV // GLA_HEADS
GLA_RANK = 16
GLA_TAU = 16.0
GLA_CHUNK = 64
N_MEM = 256
XA_HEADS = 4
XA_HD = D_MODEL // XA_HEADS
D_FF = 4 * D_MODEL
EPS = 1e-6

SUBLANES = 8
LANES = 128
MXU_DIM = 256
VMEM_LIMIT_BYTES = 56 * 1024 * 1024
MASKED_SCORE = -1e30
LOG2_E = 1.4426950408889634
ROW_TILE = 512
MLP_STAGE_ELEMS = 512 * 1024


class _Group(NamedTuple):
    nb: int
    t: int

    @property
    def shape(self):
        return (self.nb, self.t, D_MODEL)

    def tile(self, bb, tt):
        return pl.BlockSpec((bb, tt, D_MODEL), lambda b, i: (b, i, 0))

    def grid(self, bb, tt):
        return (self.nb // bb, self.t // tt)


def _params(n_axes):
    return pltpu.CompilerParams(
        dimension_semantics=("arbitrary",) * n_axes,
        vmem_limit_bytes=VMEM_LIMIT_BYTES,
    )


def _layer_spec(shape, layer):
    zeros = (0,) * len(shape)
    return pl.BlockSpec((1,) + tuple(shape), lambda *_: (layer,) + zeros,
                        pipeline_mode=pl.Buffered(1))


def _rmsnorm(x, g):
    ms = jnp.mean(x * x, axis=-1, keepdims=True)
    return x * lax.rsqrt(ms + EPS) * g


def _mm(a, w):
    return jnp.dot(a.astype(BF16), w, preferred_element_type=F32)


def _softplus(z):
    return jnp.maximum(z, 0.0) + jnp.log1p(jnp.exp(-jnp.abs(z)))


def _memkv_kernel(mem_ref, g_ref, wk_ref, wv_ref, k_ref, v_ref, kb_ref, vb_ref, *, nbm):
    mn = _rmsnorm(mem_ref[...], g_ref[0]).astype(BF16)
    k = jnp.dot(mn, wk_ref[0], preferred_element_type=F32)
    v = jnp.dot(mn, wv_ref[0], preferred_element_type=F32)
    k_ref[0] = k.reshape(nbm, N_MEM, XA_HEADS, XA_HD)
    v_ref[0] = v.reshape(nbm, N_MEM, XA_HEADS, XA_HD)
    kb_ref[0] = k.astype(BF16)
    vb_ref[0] = v.astype(BF16)


def _mem_kv(mem2d, norm_g, wk, wv, nbm=2):
    n = mem2d.shape[0]
    nb = n // N_MEM
    tm = nbm * N_MEM
    row = pl.BlockSpec((tm, D_MODEL), lambda l, i: (i, 0))
    per_layer = lambda shape: pl.BlockSpec((1,) + shape, lambda l, i: (l, 0, 0))
    out5 = pl.BlockSpec((1, nbm, N_MEM, XA_HEADS, XA_HD), lambda l, i: (l, i, 0, 0, 0))
    out3 = pl.BlockSpec((1, tm, D_MODEL), lambda l, i: (l, i, 0))
    return pl.pallas_call(
        functools.partial(_memkv_kernel, nbm=nbm),
        grid=(DEPTH, n // tm),
        in_specs=[row, per_layer((1, D_MODEL)), per_layer((D_MODEL, D_MODEL)),
                  per_layer((D_MODEL, D_MODEL))],
        out_specs=[out5, out5, out3, out3],
        out_shape=[jax.ShapeDtypeStruct((DEPTH, nb, N_MEM, XA_HEADS, XA_HD), F32)] * 2
        + [jax.ShapeDtypeStruct((DEPTH, n, D_MODEL), BF16)] * 2,
        compiler_params=_params(2),
        name="mem_kv",
    )(mem2d, norm_g, wk, wv)


def _stream_cast(src, dst, stage, sem, rows):
    n_chunks = src.shape[0] // rows

    def copy(k):
        return pltpu.make_async_copy(src.at[pl.ds(k * rows, rows)], stage.at[k % 2], sem.at[k % 2])

    copy(0).start()
    for k in range(n_chunks):
        if k + 1 < n_chunks:
            copy(k + 1).start()
        copy(k).wait()
        dst[k * rows:(k + 1) * rows, :] = stage[k % 2].astype(BF16)


def _mlp_kernel(xa_ref, xb_ref, g_ref, w1_hbm, w2_hbm, gf_ref, oa_ref, ob_ref,
                w1_ref, w2_ref, stage1_ref, stage2_ref, sem1, sem2, *,
                layer, steps_a, final, f_chunk):
    p = pl.program_id(0)

    @pl.when(p == 0)
    def _():
        _stream_cast(w1_hbm.at[layer], w1_ref, stage1_ref, sem1, stage1_ref.shape[1])
        _stream_cast(w2_hbm.at[layer], w2_ref, stage2_ref, sem2, stage2_ref.shape[1])

    def apply(x_ref, o_ref):
        x = x_ref[...].reshape(-1, D_MODEL)
        xn = _rmsnorm(x, g_ref[0]).astype(BF16)
        acc = x
        for c in range(D_FF // f_chunk):
            cols = slice(c * f_chunk, (c + 1) * f_chunk)
            hid = jnp.dot(xn, w1_ref[:, cols], preferred_element_type=F32)
            act = jnp.square(jnp.maximum(hid, 0.0)).astype(BF16)
            acc = acc + jnp.dot(act, w2_ref[cols, :], preferred_element_type=F32)
        if final:
            acc = _rmsnorm(acc, gf_ref[...])
        o_ref[...] = acc.reshape(o_ref.shape)

    pl.when(p < steps_a)(lambda: apply(xa_ref, oa_ref))
    pl.when(p >= steps_a)(lambda: apply(xb_ref, ob_ref))


def _mlp(xa, group_a, tile_a, xb, group_b, tile_b, w, layer, f_chunk=1024):
    final = layer == DEPTH - 1
    t_steps_a = group_a.grid(*tile_a)[1]
    t_steps_b = group_b.grid(*tile_b)[1]
    steps_a = group_a.grid(*tile_a)[0] * t_steps_a
    steps_b = group_b.grid(*tile_b)[0] * t_steps_b

    def index_a(p):
        q = jnp.minimum(p, steps_a - 1)
        return (q // t_steps_a, q % t_steps_a, 0)

    def index_b(p):
        q = jnp.maximum(p - steps_a, 0)
        return (q // t_steps_b, q % t_steps_b, 0)

    spec_a = pl.BlockSpec(tuple(tile_a) + (D_MODEL,), index_a)
    spec_b = pl.BlockSpec(tuple(tile_b) + (D_MODEL,), index_b)
    hbm = pl.BlockSpec(memory_space=pl.ANY)
    return pl.pallas_call(
        functools.partial(_mlp_kernel, layer=layer, steps_a=steps_a, final=final, f_chunk=f_chunk),
        grid=(steps_a + steps_b,),
        in_specs=[spec_a, spec_b, _layer_spec((1, D_MODEL), layer), hbm, hbm,
                  pl.BlockSpec((1, D_MODEL), lambda p: (0, 0))],
        out_specs=[spec_a, spec_b],
        out_shape=[jax.ShapeDtypeStruct(group_a.shape, F32),
                   jax.ShapeDtypeStruct(group_b.shape, F32)],
        scratch_shapes=[pltpu.VMEM((D_MODEL, D_FF), BF16), pltpu.VMEM((D_FF, D_MODEL), BF16),
                        pltpu.VMEM((2, MLP_STAGE_ELEMS // D_FF, D_FF), F32),
                        pltpu.VMEM((2, MLP_STAGE_ELEMS // D_MODEL, D_MODEL), F32),
                        pltpu.SemaphoreType.DMA((2,)), pltpu.SemaphoreType.DMA((2,))],
        compiler_params=_params(1),
        name="mlp",
    )(xa, xb, w["norm_mlp_g"], w["mlp_w1"], w["mlp_w2"], w["final_norm_g"])


def _xattn_kernel(x_ref, g_ref, wq_ref, wo_ref, k_ref, v_ref, o_ref, att_ref, *, bb, tt, packed_heads):
    n = bb * tt
    x = x_ref[...].reshape(n, D_MODEL)
    xn = _rmsnorm(x, g_ref[0])
    q = _mm(xn, wq_ref[0]) * (XA_HD ** -0.5 * LOG2_E)

    def softmax(s):
        e = jnp.exp2(s - jnp.max(s, axis=-1, keepdims=True))
        return (e / jnp.sum(e, axis=-1, keepdims=True)).astype(BF16)

    nt_dims = (((1,), (1,)), ((), ()))
    if packed_heads:
        rows_b = XA_HEADS * tt
        shape = (bb * rows_b, N_MEM * XA_HEADS)
        q_head = (lax.broadcasted_iota(jnp.int32, shape, 0) % rows_b) // tt
        kv_head = lax.broadcasted_iota(jnp.int32, shape, 1) % XA_HEADS
        same_head = q_head == kv_head
        scores = []
        for b in range(bb):
            qs = jnp.concatenate(
                [q[b * tt:(b + 1) * tt, h * XA_HD:(h + 1) * XA_HD] for h in range(XA_HEADS)],
                axis=0).astype(BF16)
            kf = k_ref[0, b].reshape(N_MEM * XA_HEADS, XA_HD).astype(BF16)
            scores.append(lax.dot_general(qs, kf, nt_dims, preferred_element_type=F32))
        s = jnp.concatenate(scores, axis=0)
        p = softmax(jnp.where(same_head, s, MASKED_SCORE))
        for b in range(bb):
            vf = v_ref[0, b].reshape(N_MEM * XA_HEADS, XA_HD).astype(BF16)
            o = jnp.dot(p[b * rows_b:(b + 1) * rows_b], vf, preferred_element_type=F32)
            for h in range(XA_HEADS):
                att_ref[b * tt:(b + 1) * tt, h * XA_HD:(h + 1) * XA_HD] = o[h * tt:(h + 1) * tt]
    else:
        qb = q.astype(BF16)
        for b in range(bb):
            rows = slice(b * tt, (b + 1) * tt)
            head_cols = [slice(h * XA_HD, (h + 1) * XA_HD) for h in range(XA_HEADS)]
            s = jnp.concatenate(
                [lax.dot_general(qb[rows, cols], k_ref[0, b, :, cols], nt_dims,
                                 preferred_element_type=F32) for cols in head_cols], axis=0)
            p = softmax(s)
            for h, cols in enumerate(head_cols):
                att_ref[rows, cols] = jnp.dot(
                    p[h * tt:(h + 1) * tt], v_ref[0, b, :, cols], preferred_element_type=F32)
    out = x + _mm(att_ref[...], wo_ref[0])
    o_ref[...] = out.reshape(o_ref.shape)


def _xattn(x, group, w, layer, k, v, bb, tt):
    packed_heads = k.ndim == 5
    if packed_heads:
        kv = pl.BlockSpec((1, bb, N_MEM, XA_HEADS, XA_HD), lambda b, i: (layer, b, 0, 0, 0))
    else:
        kv = pl.BlockSpec((1, bb, N_MEM, D_MODEL), lambda b, i: (layer, b, 0, 0))
    return pl.pallas_call(
        functools.partial(_xattn_kernel, bb=bb, tt=tt, packed_heads=packed_heads),
        grid=group.grid(bb, tt),
        in_specs=[group.tile(bb, tt), _layer_spec((1, D_MODEL), layer),
                  _layer_spec((D_MODEL, D_MODEL), layer), _layer_spec((D_MODEL, D_MODEL), layer),
                  kv, kv],
        out_specs=group.tile(bb, tt),
        out_shape=jax.ShapeDtypeStruct(group.shape, F32),
        scratch_shapes=[pltpu.VMEM((bb * tt, D_MODEL), F32)],
        compiler_params=_params(2),
        name="xattn",
    )(x, w["norm_xa_g"], w["xa_wq"], w["xa_wo"], k, v)


def _rg_kernel(x_ref, buf_ref, h0_ref, g_ref, win_ref, cw_ref, cb_ref, wg_ref, ba_ref, bx_ref,
               lam_ref, wout_ref, o_ref, nbuf_ref, hl_ref, u_ref, hc_ref, tm_ref, *, bb, tt, slabs):
    n = bb * tt
    pad = SUBLANES
    n_prev = CONV_W - 1
    assert (bb == SUBLANES) if slabs else (tt == SUBLANES)
    i = pl.program_id(1)

    @pl.when(i == 0)
    def _():
        u_ref[...] = jnp.zeros((bb, pad, D_RNN), F32)
        if slabs:
            for j in range(n_prev):
                u_ref[pad - n_prev + j] = buf_ref[0, :, j, :]
        else:
            u_ref[:, pad - n_prev:pad, :] = buf_ref[0]
        hc_ref[...] = h0_ref[0]

    if slabs:
        x = jnp.concatenate([x_ref[:, t, :] for t in range(tt)], axis=0)
    else:
        x = x_ref[...].reshape(n, D_MODEL)
    xn = _rmsnorm(x, g_ref[0])
    yx = _mm(xn, win_ref[0])
    gate = jax.nn.gelu(yx[:, :D_RNN])
    u = yx[:, D_RNN:]

    def tap(j):
        return cw_ref[0, CONV_W - 1 - j:CONV_W - j, :]

    if slabs:
        steps = [u_ref[pad - n_prev + j] for j in range(n_prev)]
        steps += [u[t * bb:(t + 1) * bb] for t in range(tt)]
        xc_steps = []
        for t in range(tt):
            acc = cb_ref[0] + tap(0) * steps[n_prev + t]
            for j in range(1, CONV_W):
                acc = acc + tap(j) * steps[n_prev + t - j]
            xc_steps.append(acc)
        xc2 = jnp.concatenate(xc_steps, axis=0)
        for j in range(n_prev):
            u_ref[pad - n_prev + j] = steps[tt + j]
            nbuf_ref[0, :, j, :] = steps[tt + j]
    else:
        u3 = u.reshape(bb, tt, D_RNN)
        ext = jnp.concatenate([u_ref[...], u3], axis=1)
        xc = cb_ref[0].reshape(1, 1, D_RNN) + tap(0).reshape(1, 1, D_RNN) * u3
        for j in range(1, CONV_W):
            xc = xc + tap(j).reshape(1, 1, D_RNN) * pltpu.roll(ext, j, 1)[:, pad:, :]
        nbuf_ref[0] = ext[:, pad + tt - n_prev:, :]
        u_ref[...] = ext[:, tt:, :]
        xc2 = xc.reshape(n, D_RNN)

    gates = [_mm(xc2[:, c * MXU_DIM:(c + 1) * MXU_DIM], wg_ref[0, c]) for c in range(D_RNN // MXU_DIM)]
    r = jax.nn.sigmoid(jnp.concatenate([gc[:, :MXU_DIM] for gc in gates], axis=1) + ba_ref[0])
    ig = jax.nn.sigmoid(jnp.concatenate([gc[:, MXU_DIM:] for gc in gates], axis=1) + bx_ref[0])
    log_a = r * ((-RG_C) * _softplus(-lam_ref[0]))
    a = jnp.exp(log_a)
    m2 = -jnp.tanh(log_a) * (a * a + 1.0)
    mult = jnp.where(m2 > 0.0, m2 * lax.rsqrt(m2), 0.0)
    bt = mult * (ig * xc2)

    if slabs:
        carry = hc_ref[:, 0, :]
        h_steps = []
        for t in range(tt):
            rows = slice(t * bb, (t + 1) * bb)
            carry = a[rows] * carry + bt[rows]
            h_steps.append(carry)
        h = jnp.concatenate(h_steps, axis=0)
        hc_ref[:, 0, :] = carry
        hl_ref[0, :, 0, :] = carry
    else:
        ag = a.reshape(bb, tt, D_RNN)
        bg = bt.reshape(bb, tt, D_RNN)
        t_idx = lax.broadcasted_iota(jnp.int32, (bb, tt, D_RNN), 1)
        s = 1
        while s < tt:
            keep = t_idx >= s
            a_s = jnp.where(keep, pltpu.roll(ag, s, 1), 1.0)
            b_s = jnp.where(keep, pltpu.roll(bg, s, 1), 0.0)
            bg = ag * b_s + bg
            ag = ag * a_s
            s *= 2
        h3 = ag * hc_ref[...] + bg
        h = h3.reshape(n, D_RNN)
        hc_ref[...] = h3[:, tt - 1:tt, :]
        hl_ref[0] = h3[:, tt - 1:tt, :]

    out = x + _mm(gate * h, wout_ref[0])
    if slabs:
        tm_ref[...] = out.reshape(tt, bb, D_MODEL)
        for b in range(bb):
            o_ref[b] = tm_ref[:, b, :]
    else:
        o_ref[...] = out.reshape(o_ref.shape)


def _rg_block(x, group, conv_buf, h0, w, layer, bb, tt):
    nb, t = group.nb, group.t
    j = layer // 2
    slabs = bb == SUBLANES and tt > SUBLANES
    tile = group.tile(bb, tt)
    buf_in = pl.BlockSpec((1, bb, CONV_W - 1, D_RNN), lambda b, i: (j, b, 0, 0))
    h_in = pl.BlockSpec((1, bb, 1, D_RNN), lambda b, i: (j, b, 0, 0))
    buf_out = pl.BlockSpec((1, bb, CONV_W - 1, D_RNN), lambda b, i: (0, b, 0, 0))
    h_out = pl.BlockSpec((1, bb, 1, D_RNN), lambda b, i: (0, b, 0, 0))
    vec = _layer_spec((1, D_RNN), j)
    return pl.pallas_call(
        functools.partial(_rg_kernel, bb=bb, tt=tt, slabs=slabs),
        grid=group.grid(bb, tt),
        in_specs=[tile, buf_in, h_in, _layer_spec((1, D_MODEL), layer),
                  _layer_spec((D_MODEL, 2 * D_RNN), j), _layer_spec((CONV_W, D_RNN), j), vec,
                  _layer_spec((D_RNN // MXU_DIM, MXU_DIM, 2 * MXU_DIM), j), vec, vec, vec,
                  _layer_spec((D_RNN, D_MODEL), j)],
        out_specs=[tile, buf_out, h_out],
        out_shape=[jax.ShapeDtypeStruct(group.shape, F32),
                   jax.ShapeDtypeStruct((1, nb, CONV_W - 1, D_RNN), F32),
                   jax.ShapeDtypeStruct((1, nb, 1, D_RNN), F32)],
        scratch_shapes=[pltpu.VMEM((bb, SUBLANES, D_RNN), F32),
                        pltpu.VMEM((bb, 1, D_RNN), F32),
                        pltpu.VMEM((tt, bb, D_MODEL) if slabs else (1, SUBLANES, LANES), F32)],
        compiler_params=_params(2),
        name="rg_block",
    )(x, conv_buf, h0, w["norm_mix_g"], w["rg_w_in"], w["rg_conv_w"], w["rg_conv_b"],
      w["rg_w_gates"], w["rg_b_a"], w["rg_b_x"], w["rg_lambda"], w["rg_w_out"])


def _block_diag_tiles(w):
    per = MXU_DIM // RG_BW
    w4 = w.reshape(RG_BLOCKS // per, per, RG_BW, RG_BW)
    eye = jnp.eye(per, dtype=w.dtype)
    t = w4[:, :, :, None, :] * eye[None, :, None, :, None]
    return t.reshape(RG_BLOCKS // per, MXU_DIM, MXU_DIM)


_GLA_MAIN = 2 * GLA_DK + 2 * GLA_DV


def _gla_kernel(*refs, bb, tt, chunk, slot, aliased):
    (x_ref, s0_ref, g_ref, win_ref, wa2_ref, ba_ref, ng_ref, wout_ref) = refs[:8]
    o_ref, s_ref, oc_ref = refs[9:] if aliased else refs[8:]
    if aliased:
        slot = 0
    n = bb * tt
    n_seg = n // chunk
    assert bb == 1 or tt == chunk
    assert chunk & (chunk - 1) == 0
    shared_state = bb == 1
    i = pl.program_id(1)

    @pl.when(i == 0)
    def _():
        for other in range(s_ref.shape[0]):
            if other != slot:
                s_ref[other] = jnp.zeros(s_ref.shape[1:], F32)
        s_ref[slot] = s0_ref[0]

    x = x_ref[...].reshape(n, D_MODEL)
    xn = _rmsnorm(x, g_ref[0]).astype(BF16)
    proj = jnp.dot(xn, win_ref[0, :, :_GLA_MAIN], preferred_element_type=F32)
    q3 = (proj[:, :GLA_DK] * (GLA_HK ** -0.5)).reshape(n_seg, chunk, GLA_DK)
    k3 = proj[:, GLA_DK:2 * GLA_DK].reshape(n_seg, chunk, GLA_DK)
    v = proj[:, 2 * GLA_DK:2 * GLA_DK + GLA_DV]
    gsilu = jax.nn.silu(proj[:, 2 * GLA_DK + GLA_DV:])
    a_lo = jnp.dot(xn, win_ref[0, :, _GLA_MAIN:], preferred_element_type=F32)
    z = _mm(a_lo, wa2_ref[0]) + ba_ref[0]
    nz = -z
    softplus_nz = jnp.maximum(nz, 0.0) + jnp.log(1.0 + jnp.exp(-jnp.abs(nz)))
    la3 = (softplus_nz * (-LOG2_E / GLA_TAU)).reshape(n_seg, chunk, GLA_DK)

    early_cast = (lambda t: t.astype(BF16)) if chunk % (2 * SUBLANES) == 0 else (lambda t: t)
    v = early_cast(v)
    t_idx = lax.broadcasted_iota(jnp.int32, (n_seg, chunk, GLA_HK), 1)
    rg = min(n, MXU_DIM)
    row = lax.broadcasted_iota(jnp.int32, (rg, rg), 0)
    col = lax.broadcasted_iota(jnp.int32, (rg, rg), 1)
    same_chunk_causal = (col <= row) & (col >= (row & (-chunk)))

    for h in range(GLA_HEADS):
        kc = slice(h * GLA_HK, (h + 1) * GLA_HK)
        vc = slice(h * GLA_HV, (h + 1) * GLA_HV)
        bcum = la3[:, :, kc]
        s = 1
        while s < chunk:
            bcum = bcum + jnp.where(t_idx >= s, pltpu.roll(bcum, s, 1), 0.0)
            s *= 2
        gl = bcum[:, chunk - 1:chunk, :]
        q_in = early_cast((q3[:, :, kc] * jnp.exp2(bcum)).reshape(n, GLA_HK))
        k_dec = k3[:, :, kc] * jnp.exp2(-bcum)
        k_in = early_cast(k_dec.reshape(n, GLA_HK))
        k_end = early_cast((k_dec * jnp.exp2(gl)).reshape(n, GLA_HK))

        for r in range(n // rg):
            rows = slice(r * rg, (r + 1) * rg)
            att = lax.dot_general(q_in[rows].astype(BF16), k_in[rows].astype(BF16),
                                  (((1,), (1,)), ((), ())), preferred_element_type=F32)
            att = jnp.where(same_chunk_causal, att, 0.0).astype(BF16)
            oc_ref[rows, vc] = jnp.dot(att, v[rows, vc].astype(BF16), preferred_element_type=F32)

        state = None
        for c in range(n_seg):
            rows = slice(c * chunk, (c + 1) * chunk)
            sb = 0 if shared_state else c
            if state is None or not shared_state:
                state = s_ref[slot, sb, h]
            oc_ref[rows, vc] = oc_ref[rows, vc] + jnp.dot(
                q_in[rows].astype(BF16), state.astype(BF16), preferred_element_type=F32)
            eg = jnp.exp2(jnp.broadcast_to(gl[c], (GLA_HK, GLA_HK))).T
            eg = jnp.concatenate([eg] * (GLA_HV // GLA_HK), axis=1)
            state = eg * state + lax.dot_general(
                k_end[rows].astype(BF16), v[rows, vc].astype(BF16),
                (((0,), (0,)), ((), ())), preferred_element_type=F32)
            if not shared_state or c == n_seg - 1:
                s_ref[slot, sb, h] = state

    o = oc_ref[...]
    heads = []
    for h in range(GLA_HEADS):
        heads.append(_rmsnorm(o[:, h * GLA_HV:(h + 1) * GLA_HV], ng_ref[0]))
    on = jnp.concatenate(heads, axis=1)
    out = x + _mm(on * gsilu, wout_ref[0])
    o_ref[...] = out.reshape(o_ref.shape)


def _gla_block(x, group, s0, s_new, w, layer, bb, tt, chunk):
    n = bb * tt
    j = layer // 2
    tile = group.tile(bb, tt)
    state = pl.BlockSpec((1, bb, GLA_HEADS, GLA_HK, GLA_HV), lambda b, i: (j, b, 0, 0, 0))
    in_specs = [tile, state, _layer_spec((1, D_MODEL), layer),
                _layer_spec((D_MODEL, _GLA_MAIN + GLA_RANK), j), _layer_spec((GLA_RANK, GLA_DK), j),
                _layer_spec((1, GLA_DK), j), _layer_spec((1, GLA_HV), j),
                _layer_spec((GLA_DV, D_MODEL), j)]
    args = [x, s0, w["norm_mix_g"], w["gla_w_in"], w["gla_w_a2"], w["gla_b_a"],
            w["gla_norm_g"], w["gla_w_out"]]
    if s_new is None:
        aliases = {}
        state_out = pl.BlockSpec((s0.shape[0], bb, GLA_HEADS, GLA_HK, GLA_HV),
                                 lambda b, i: (0, b, 0, 0, 0))
    else:
        in_specs.append(pl.BlockSpec(memory_space=pl.ANY))
        args.append(s_new)
        aliases = {len(args) - 1: 1}
        state_out = state
    return pl.pallas_call(
        functools.partial(_gla_kernel, bb=bb, tt=tt, chunk=chunk, slot=j,
                          aliased=s_new is not None),
        grid=group.grid(bb, tt),
        in_specs=in_specs,
        out_specs=[tile, state_out],
        out_shape=[jax.ShapeDtypeStruct(group.shape, F32), jax.ShapeDtypeStruct(s0.shape, F32)],
        input_output_aliases=aliases,
        scratch_shapes=[pltpu.VMEM((n, GLA_DV), F32)],
        compiler_params=_params(2),
        name="gla_block",
    )(*args)


class _Stream(NamedTuple):
    x: jax.Array
    group: _Group
    mem_k: jax.Array
    mem_v: jax.Array
    rg_h: jax.Array
    rg_conv: jax.Array
    gla_s: jax.Array
    tiles: dict


def _run_layers(streams, w):
    xs = [s.x for s in streams]
    hs = [[] for _ in streams]
    convs = [[] for _ in streams]
    s_new = [None for _ in streams]
    for layer in range(DEPTH):
        for n, s in enumerate(streams):
            if layer % 2 == 0:
                xs[n], cb, hl = _rg_block(xs[n], s.group, s.rg_conv, s.rg_h, w, layer,
                                          *s.tiles["rg"])
                convs[n].append(cb)
                hs[n].append(hl)
            else:
                gla_tile = s.tiles["gla"] if s_new[n] is None else s.tiles["gla_in_place"]
                xs[n], s_new[n] = _gla_block(xs[n], s.group, s.gla_s, s_new[n], w, layer,
                                             *gla_tile)
            xs[n] = _xattn(xs[n], s.group, w, layer, s.mem_k, s.mem_v, *s.tiles["xattn"])
        a, b = streams
        xs = list(_mlp(xs[0], a.group, a.tiles["mlp"], xs[1], b.group, b.tiles["mlp"], w, layer))
    results = []
    for n, s in enumerate(streams):
        h_all = jnp.concatenate(hs[n], axis=0).reshape(len(hs[n]), s.group.nb, D_RNN)
        results.append((xs[n], h_all, jnp.concatenate(convs[n], axis=0), s_new[n]))
    return results


def kernel(x_prompt, x_sample, mem_prompt, state_rglru_h, state_rglru_conv, state_gla_S, cache_mem_k, cache_mem_v, norm_mix_g, norm_xa_g, norm_mem_g, norm_mlp_g, final_norm_g, rg_w_in, rg_conv_w, rg_conv_b, rg_w_a, rg_b_a, rg_w_x, rg_b_x, rg_lambda, rg_w_out, gla_w_in, gla_w_a2, gla_b_a, gla_norm_g, gla_w_out, xa_wq, xa_wk, xa_wv, xa_wo, mlp_w1, mlp_w2):
    batch, seq, _ = x_prompt.shape
    dec_batch, dec_seq, _ = x_sample.shape
    n_a = rg_w_in.shape[0]
    n_b = gla_w_in.shape[0]
    assert batch == SUBLANES and dec_seq == SUBLANES

    def rows(p):
        return p.reshape(p.shape[0], 1, p.shape[1])

    gates = jnp.concatenate(
        [jax.vmap(_block_diag_tiles)(rg_w_a), jax.vmap(_block_diag_tiles)(rg_w_x)], axis=-1)
    w = dict(
        norm_mix_g=rows(norm_mix_g), norm_xa_g=rows(norm_xa_g), norm_mlp_g=rows(norm_mlp_g),
        final_norm_g=final_norm_g.reshape(1, D_MODEL),
        rg_w_in=rg_w_in.astype(BF16), rg_conv_w=rg_conv_w, rg_conv_b=rows(rg_conv_b),
        rg_w_gates=gates.astype(BF16), rg_b_a=rows(rg_b_a), rg_b_x=rows(rg_b_x),
        rg_lambda=rows(rg_lambda), rg_w_out=rg_w_out.astype(BF16),
        gla_w_in=gla_w_in.astype(BF16), gla_w_a2=gla_w_a2.astype(BF16),
        gla_b_a=rows(gla_b_a), gla_norm_g=rows(gla_norm_g), gla_w_out=gla_w_out.astype(BF16),
        xa_wq=xa_wq.astype(BF16), xa_wo=xa_wo.astype(BF16),
        mlp_w1=mlp_w1, mlp_w2=mlp_w2,
    )

    mem2d = mem_prompt.reshape(batch * N_MEM, D_MODEL)
    mem_k_prompt, mem_v_prompt, mkb, mvb = _mem_kv(
        mem2d, rows(norm_mem_g), xa_wk.astype(BF16), xa_wv.astype(BF16))
    prompt_tiles = dict(rg=(SUBLANES, ROW_TILE // SUBLANES), gla=(1, ROW_TILE, GLA_CHUNK),
                        gla_in_place=(1, ROW_TILE, GLA_CHUNK), xattn=(1, ROW_TILE),
                        mlp=(1, ROW_TILE))
    prompt = _Stream(
        x_prompt, _Group(batch, seq),
        mkb.reshape(DEPTH, batch, N_MEM, D_MODEL), mvb.reshape(DEPTH, batch, N_MEM, D_MODEL),
        jnp.zeros((n_a, batch, 1, D_RNN), F32), jnp.zeros((n_a, batch, CONV_W - 1, D_RNN), F32),
        jnp.zeros((n_b, batch, GLA_HEADS, GLA_HK, GLA_HV), F32), prompt_tiles)

    sample_chunk = GLA_CHUNK if dec_seq % GLA_CHUNK == 0 else dec_seq
    sample_tiles = dict(rg=(32, dec_seq), gla=(8, dec_seq, sample_chunk),
                        gla_in_place=(16, dec_seq, sample_chunk), xattn=(8, dec_seq),
                        mlp=(ROW_TILE // dec_seq, dec_seq))
    sample = _Stream(
        x_sample, _Group(dec_batch, dec_seq), cache_mem_k, cache_mem_v,
        state_rglru_h.reshape(n_a, dec_batch, 1, D_RNN), state_rglru_conv, state_gla_S,
        sample_tiles)

    (y_p, h_p, conv_p, s_p), (y_s, h_s, conv_s, s_s) = _run_layers((prompt, sample), w)
    return (y_p, y_s, mem_k_prompt, mem_v_prompt, h_p, conv_p, s_p, h_s, conv_s, s_s)
```

```python
import functools
from typing import NamedTuple

import jax
import jax.numpy as jnp
from jax import lax
from jax.experimental import pallas as pl
from jax.experimental.pallas import tpu as pltpu

F32 = jnp.float32
BF16 = jnp.bfloat16

D_MODEL = 1024
DEPTH = 4
D_RNN = D_MODEL
RG_BLOCKS = 16
RG_BW = D_RNN // RG_BLOCKS
CONV_W = 4
RG_C = 8.0
GLA_HEADS = 4
GLA_DK = D_MODEL // 2
GLA_DV = D_MODEL
GLA_HK = GLA_DK // GLA_HEADS
GLA_HV = GLA_DV // GLA_HEADS
GLA_RANK = 16
GLA_TAU = 16.0
GLA_CHUNK = 64
N_MEM = 256
XA_HEADS = 4
XA_HD = D_MODEL // XA_HEADS
D_FF = 4 * D_MODEL
EPS = 1e-6

SUBLANES = 8
LANES = 128
MXU_DIM = 256
VMEM_LIMIT_BYTES = 56 * 1024 * 1024
MASKED_SCORE = -1e30
LOG2_E = 1.4426950408889634
ROW_TILE = 512
MLP_STAGE_ELEMS = 512 * 1024


class _Group(NamedTuple):
    nb: int
    t: int

    @property
    def shape(self):
        return (self.nb, self.t, D_MODEL)

    def tile(self, bb, tt):
        return pl.BlockSpec((bb, tt, D_MODEL), lambda b, i: (b, i, 0))

    def grid(self, bb, tt):
        return (self.nb // bb, self.t // tt)


def _params(n_axes):
    return pltpu.CompilerParams(
        dimension_semantics=("arbitrary",) * n_axes,
        vmem_limit_bytes=VMEM_LIMIT_BYTES,
    )


def _layer_spec(shape, layer):
    zeros = (0,) * len(shape)
    return pl.BlockSpec((1,) + tuple(shape), lambda *_: (layer,) + zeros,
                        pipeline_mode=pl.Buffered(1))


def _rmsnorm(x, g):
    ms = jnp.mean(x * x, axis=-1, keepdims=True)
    return x * lax.rsqrt(ms + EPS) * g


def _mm(a, w):
    return jnp.dot(a.astype(BF16), w, preferred_element_type=F32)


def _softplus(z):
    return jnp.maximum(z, 0.0) + jnp.log1p(jnp.exp(-jnp.abs(z)))


def _memkv_kernel(mem_ref, g_ref, wk_ref, wv_ref, k_ref, v_ref, kb_ref, vb_ref, *, nbm):
    mn = _rmsnorm(mem_ref[...], g_ref[0]).astype(BF16)
    k = jnp.dot(mn, wk_ref[0], preferred_element_type=F32)
    v = jnp.dot(mn, wv_ref[0], preferred_element_type=F32)
    k_ref[0] = k.reshape(nbm, N_MEM, XA_HEADS, XA_HD)
    v_ref[0] = v.reshape(nbm, N_MEM, XA_HEADS, XA_HD)
    kb_ref[0] = k.astype(BF16)
    vb_ref[0] = v.astype(BF16)


def _mem_kv(mem2d, norm_g, wk, wv, nbm=2):
    n = mem2d.shape[0]
    nb = n // N_MEM
    tm = nbm * N_MEM
    row = pl.BlockSpec((tm, D_MODEL), lambda l, i: (i, 0))
    per_layer = lambda shape: pl.BlockSpec((1,) + shape, lambda l, i: (l, 0, 0))
    out5 = pl.BlockSpec((1, nbm, N_MEM, XA_HEADS, XA_HD), lambda l, i: (l, i, 0, 0, 0))
    out3 = pl.BlockSpec((1, tm, D_MODEL), lambda l, i: (l, i, 0))
    return pl.pallas_call(
        functools.partial(_memkv_kernel, nbm=nbm),
        grid=(DEPTH, n // tm),
        in_specs=[row, per_layer((1, D_MODEL)), per_layer((D_MODEL, D_MODEL)),
                  per_layer((D_MODEL, D_MODEL))],
        out_specs=[out5, out5, out3, out3],
        out_shape=[jax.ShapeDtypeStruct((DEPTH, nb, N_MEM, XA_HEADS, XA_HD), F32)] * 2
        + [jax.ShapeDtypeStruct((DEPTH, n, D_MODEL), BF16)] * 2,
        compiler_params=_params(2),
        name="mem_kv",
    )(mem2d, norm_g, wk, wv)


def _stream_cast(src, dst, stage, sem, rows):
    n_chunks = src.shape[0] // rows

    def copy(k):
        return pltpu.make_async_copy(src.at[pl.ds(k * rows, rows)], stage.at[k % 2], sem.at[k % 2])

    copy(0).start()
    for k in range(n_chunks):
        if k + 1 < n_chunks:
            copy(k + 1).start()
        copy(k).wait()
        dst[k * rows:(k + 1) * rows, :] = stage[k % 2].astype(BF16)


def _mlp_kernel(xa_ref, xb_ref, g_ref, w1_hbm, w2_hbm, gf_ref, oa_ref, ob_ref,
                w1_ref, w2_ref, stage1_ref, stage2_ref, sem1, sem2, *,
                layer, steps_a, final, f_chunk):
    p = pl.program_id(0)

    @pl.when(p == 0)
    def _():
        _stream_cast(w1_hbm.at[layer], w1_ref, stage1_ref, sem1, stage1_ref.shape[1])
        _stream_cast(w2_hbm.at[layer], w2_ref, stage2_ref, sem2, stage2_ref.shape[1])

    def apply(x_ref, o_ref):
        x = x_ref[...].reshape(-1, D_MODEL)
        xn = _rmsnorm(x, g_ref[0]).astype(BF16)
        acc = x
        for c in range(D_FF // f_chunk):
            cols = slice(c * f_chunk, (c + 1) * f_chunk)
            hid = jnp.dot(xn, w1_ref[:, cols], preferred_element_type=F32)
            act = jnp.square(jnp.maximum(hid, 0.0)).astype(BF16)
            acc = acc + jnp.dot(act, w2_ref[cols, :], preferred_element_type=F32)
        if final:
            acc = _rmsnorm(acc, gf_ref[...])
        o_ref[...] = acc.reshape(o_ref.shape)

    pl.when(p < steps_a)(lambda: apply(xa_ref, oa_ref))
    pl.when(p >= steps_a)(lambda: apply(xb_ref, ob_ref))


def _mlp(xa, group_a, tile_a, xb, group_b, tile_b, w, layer, f_chunk=1024):
    final = layer == DEPTH - 1
    t_steps_a = group_a.grid(*tile_a)[1]
    t_steps_b = group_b.grid(*tile_b)[1]
    steps_a = group_a.grid(*tile_a)[0] * t_steps_a
    steps_b = group_b.grid(*tile_b)[0] * t_steps_b

    def index_a(p):
        q = jnp.minimum(p, steps_a - 1)
        return (q // t_steps_a, q % t_steps_a, 0)

    def index_b(p):
        q = jnp.maximum(p - steps_a, 0)
        return (q // t_steps_b, q % t_steps_b, 0)

    spec_a = pl.BlockSpec(tuple(tile_a) + (D_MODEL,), index_a)
    spec_b = pl.BlockSpec(tuple(tile_b) + (D_MODEL,), index_b)
    hbm = pl.BlockSpec(memory_space=pl.ANY)
    return pl.pallas_call(
        functools.partial(_mlp_kernel, layer=layer, steps_a=steps_a, final=final, f_chunk=f_chunk),
        grid=(steps_a + steps_b,),
        in_specs=[spec_a, spec_b, _layer_spec((1, D_MODEL), layer), hbm, hbm,
                  pl.BlockSpec((1, D_MODEL), lambda p: (0, 0))],
        out_specs=[spec_a, spec_b],
        out_shape=[jax.ShapeDtypeStruct(group_a.shape, F32),
                   jax.ShapeDtypeStruct(group_b.shape, F32)],
        scratch_shapes=[pltpu.VMEM((D_MODEL, D_FF), BF16), pltpu.VMEM((D_FF, D_MODEL), BF16),
                        pltpu.VMEM((2, MLP_STAGE_ELEMS // D_FF, D_FF), F32),
                        pltpu.VMEM((2, MLP_STAGE_ELEMS // D_MODEL, D_MODEL), F32),
                        pltpu.SemaphoreType.DMA((2,)), pltpu.SemaphoreType.DMA((2,))],
        compiler_params=_params(1),
        name="mlp",
    )(xa, xb, w["norm_mlp_g"], w["mlp_w1"], w["mlp_w2"], w["final_norm_g"])


def _xattn_kernel(x_ref, g_ref, wq_ref, wo_ref, k_ref, v_ref, o_ref, att_ref, *, bb, tt, packed_heads):
    n = bb * tt
    x = x_ref[...].reshape(n, D_MODEL)
    xn = _rmsnorm(x, g_ref[0])
    q = _mm(xn, wq_ref[0]) * (XA_HD ** -0.5 * LOG2_E)

    def softmax(s):
        e = jnp.exp2(s - jnp.max(s, axis=-1, keepdims=True))
        return (e / jnp.sum(e, axis=-1, keepdims=True)).astype(BF16)

    nt_dims = (((1,), (1,)), ((), ()))
    if packed_heads:
        rows_b = XA_HEADS * tt
        shape = (bb * rows_b, N_MEM * XA_HEADS)
        q_head = (lax.broadcasted_iota(jnp.int32, shape, 0) % rows_b) // tt
        kv_head = lax.broadcasted_iota(jnp.int32, shape, 1) % XA_HEADS
        same_head = q_head == kv_head
        scores = []
        for b in range(bb):
            qs = jnp.concatenate(
                [q[b * tt:(b + 1) * tt, h * XA_HD:(h + 1) * XA_HD] for h in range(XA_HEADS)],
                axis=0).astype(BF16)
            kf = k_ref[0, b].reshape(N_MEM * XA_HEADS, XA_HD).astype(BF16)
            scores.append(lax.dot_general(qs, kf, nt_dims, preferred_element_type=F32))
        s = jnp.concatenate(scores, axis=0)
        p = softmax(jnp.where(same_head, s, MASKED_SCORE))
        for b in range(bb):
            vf = v_ref[0, b].reshape(N_MEM * XA_HEADS, XA_HD).astype(BF16)
            o = jnp.dot(p[b * rows_b:(b + 1) * rows_b], vf, preferred_element_type=F32)
            for h in range(XA_HEADS):
                att_ref[b * tt:(b + 1) * tt, h * XA_HD:(h + 1) * XA_HD] = o[h * tt:(h + 1) * tt]
    else:
        qb = q.astype(BF16)
        for b in range(bb):
            rows = slice(b * tt, (b + 1) * tt)
            head_cols = [slice(h * XA_HD, (h + 1) * XA_HD) for h in range(XA_HEADS)]
            s = jnp.concatenate(
                [lax.dot_general(qb[rows, cols], k_ref[0, b, :, cols], nt_dims,
                                 preferred_element_type=F32) for cols in head_cols], axis=0)
            p = softmax(s)
            for h, cols in enumerate(head_cols):
                att_ref[rows, cols] = jnp.dot(
                    p[h * tt:(h + 1) * tt], v_ref[0, b, :, cols], preferred_element_type=F32)
    out = x + _mm(att_ref[...], wo_ref[0])
    o_ref[...] = out.reshape(o_ref.shape)


def _xattn(x, group, w, layer, k, v, bb, tt):
    packed_heads = k.ndim == 5
    if packed_heads:
        kv = pl.BlockSpec((1, bb, N_MEM, XA_HEADS, XA_HD), lambda b, i: (layer, b, 0, 0, 0))
    else:
        kv = pl.BlockSpec((1, bb, N_MEM, D_MODEL), lambda b, i: (layer, b, 0, 0))
    return pl.pallas_call(
        functools.partial(_xattn_kernel, bb=bb, tt=tt, packed_heads=packed_heads),
        grid=group.grid(bb, tt),
        in_specs=[group.tile(bb, tt), _layer_spec((1, D_MODEL), layer),
                  _layer_spec((D_MODEL, D_MODEL), layer), _layer_spec((D_MODEL, D_MODEL), layer),
                  kv, kv],
        out_specs=group.tile(bb, tt),
        out_shape=jax.ShapeDtypeStruct(group.shape, F32),
        scratch_shapes=[pltpu.VMEM((bb * tt, D_MODEL), F32)],
        compiler_params=_params(2),
        name="xattn",
    )(x, w["norm_xa_g"], w["xa_wq"], w["xa_wo"], k, v)


def _rg_kernel(x_ref, buf_ref, h0_ref, g_ref, win_ref, cw_ref, cb_ref, wg_ref, ba_ref, bx_ref,
               lam_ref, wout_ref, o_ref, nbuf_ref, hl_ref, u_ref, hc_ref, tm_ref, *, bb, tt, slabs):
    n = bb * tt
    pad = SUBLANES
    n_prev = CONV_W - 1
    assert (bb == SUBLANES) if slabs else (tt == SUBLANES)
    i = pl.program_id(1)

    @pl.when(i == 0)
    def _():
        u_ref[...] = jnp.zeros((bb, pad, D_RNN), F32)
        if slabs:
            for j in range(n_prev):
                u_ref[pad - n_prev + j] = buf_ref[0, :, j, :]
        else:
            u_ref[:, pad - n_prev:pad, :] = buf_ref[0]
        hc_ref[...] = h0_ref[0]

    if slabs:
        x = jnp.concatenate([x_ref[:, t, :] for t in range(tt)], axis=0)
    else:
        x = x_ref[...].reshape(n, D_MODEL)
    xn = _rmsnorm(x, g_ref[0])
    yx = _mm(xn, win_ref[0])
    gate = jax.nn.gelu(yx[:, :D_RNN])
    u = yx[:, D_RNN:]

    def tap(j):
        return cw_ref[0, CONV_W - 1 - j:CONV_W - j, :]

    if slabs:
        steps = [u_ref[pad - n_prev + j] for j in range(n_prev)]
        steps += [u[t * bb:(t + 1) * bb] for t in range(tt)]
        xc_steps = []
        for t in range(tt):
            acc = cb_ref[0] + tap(0) * steps[n_prev + t]
            for j in range(1, CONV_W):
                acc = acc + tap(j) * steps[n_prev + t - j]
            xc_steps.append(acc)
        xc2 = jnp.concatenate(xc_steps, axis=0)
        for j in range(n_prev):
            u_ref[pad - n_prev + j] = steps[tt + j]
            nbuf_ref[0, :, j, :] = steps[tt + j]
    else:
        u3 = u.reshape(bb, tt, D_RNN)
        ext = jnp.concatenate([u_ref[...], u3], axis=1)
        xc = cb_ref[0].reshape(1, 1, D_RNN) + tap(0).reshape(1, 1, D_RNN) * u3
        for j in range(1, CONV_W):
            xc = xc + tap(j).reshape(1, 1, D_RNN) * pltpu.roll(ext, j, 1)[:, pad:, :]
        nbuf_ref[0] = ext[:, pad + tt - n_prev:, :]
        u_ref[...] = ext[:, tt:, :]
        xc2 = xc.reshape(n, D_RNN)

    gates = [_mm(xc2[:, c * MXU_DIM:(c + 1) * MXU_DIM], wg_ref[0, c]) for c in range(D_RNN // MXU_DIM)]
    r = jax.nn.sigmoid(jnp.concatenate([gc[:, :MXU_DIM] for gc in gates], axis=1) + ba_ref[0])
    ig = jax.nn.sigmoid(jnp.concatenate([gc[:, MXU_DIM:] for gc in gates], axis=1) + bx_ref[0])
    log_a = r * ((-RG_C) * _softplus(-lam_ref[0]))
    a = jnp.exp(log_a)
    m2 = -jnp.tanh(log_a) * (a * a + 1.0)
    mult = jnp.where(m2 > 0.0, m2 * lax.rsqrt(m2), 0.0)
    bt = mult * (ig * xc2)

    if slabs:
        carry = hc_ref[:, 0, :]
        h_steps = []
        for t in range(tt):
            rows = slice(t * bb, (t + 1) * bb)
            carry = a[rows] * carry + bt[rows]
            h_steps.append(carry)
        h = jnp.concatenate(h_steps, axis=0)
        hc_ref[:, 0, :] = carry
        hl_ref[0, :, 0, :] = carry
    else:
        ag = a.reshape(bb, tt, D_RNN)
        bg = bt.reshape(bb, tt, D_RNN)
        t_idx = lax.broadcasted_iota(jnp.int32, (bb, tt, D_RNN), 1)
        s = 1
        while s < tt:
            keep = t_idx >= s
            a_s = jnp.where(keep, pltpu.roll(ag, s, 1), 1.0)
            b_s = jnp.where(keep, pltpu.roll(bg, s, 1), 0.0)
            bg = ag * b_s + bg
            ag = ag * a_s
            s *= 2
        h3 = ag * hc_ref[...] + bg
        h = h3.reshape(n, D_RNN)
        hc_ref[...] = h3[:, tt - 1:tt, :]
        hl_ref[0] = h3[:, tt - 1:tt, :]

    out = x + _mm(gate * h, wout_ref[0])
    if slabs:
        tm_ref[...] = out.reshape(tt, bb, D_MODEL)
        for b in range(bb):
            o_ref[b] = tm_ref[:, b, :]
    else:
        o_ref[...] = out.reshape(o_ref.shape)


def _rg_block(x, group, conv_buf, h0, w, layer, bb, tt):
    nb, t = group.nb, group.t
    j = layer // 2
    slabs = bb == SUBLANES and tt > SUBLANES
    tile = group.tile(bb, tt)
    buf_in = pl.BlockSpec((1, bb, CONV_W - 1, D_RNN), lambda b, i: (j, b, 0, 0))
    h_in = pl.BlockSpec((1, bb, 1, D_RNN), lambda b, i: (j, b, 0, 0))
    buf_out = pl.BlockSpec((1, bb, CONV_W - 1, D_RNN), lambda b, i: (0, b, 0, 0))
    h_out = pl.BlockSpec((1, bb, 1, D_RNN), lambda b, i: (0, b, 0, 0))
    vec = _layer_spec((1, D_RNN), j)
    return pl.pallas_call(
        functools.partial(_rg_kernel, bb=bb, tt=tt, slabs=slabs),
        grid=group.grid(bb, tt),
        in_specs=[tile, buf_in, h_in, _layer_spec((1, D_MODEL), layer),
                  _layer_spec((D_MODEL, 2 * D_RNN), j), _layer_spec((CONV_W, D_RNN), j), vec,
                  _layer_spec((D_RNN // MXU_DIM, MXU_DIM, 2 * MXU_DIM), j), vec, vec, vec,
                  _layer_spec((D_RNN, D_MODEL), j)],
        out_specs=[tile, buf_out, h_out],
        out_shape=[jax.ShapeDtypeStruct(group.shape, F32),
                   jax.ShapeDtypeStruct((1, nb, CONV_W - 1, D_RNN), F32),
                   jax.ShapeDtypeStruct((1, nb, 1, D_RNN), F32)],
        scratch_shapes=[pltpu.VMEM((bb, SUBLANES, D_RNN), F32),
                        pltpu.VMEM((bb, 1, D_RNN), F32),
                        pltpu.VMEM((tt, bb, D_MODEL) if slabs else (1, SUBLANES, LANES), F32)],
        compiler_params=_params(2),
        name="rg_block",
    )(x, conv_buf, h0, w["norm_mix_g"], w["rg_w_in"], w["rg_conv_w"], w["rg_conv_b"],
      w["rg_w_gates"], w["rg_b_a"], w["rg_b_x"], w["rg_lambda"], w["rg_w_out"])


def _block_diag_tiles(w):
    per = MXU_DIM // RG_BW
    w4 = w.reshape(RG_BLOCKS // per, per, RG_BW, RG_BW)
    eye = jnp.eye(per, dtype=w.dtype)
    t = w4[:, :, :, None, :] * eye[None, :, None, :, None]
    return t.reshape(RG_BLOCKS // per, MXU_DIM, MXU_DIM)


_GLA_MAIN = 2 * GLA_DK + 2 * GLA_DV


def _gla_kernel(*refs, bb, tt, chunk, mixer, aliased):
    (x_ref, s0_ref, g_ref, win_hbm, wa2_ref, ba_ref, ng_ref, wout_ref) = refs[:8]
    o_ref, s_ref, oc_ref, win_ref, stage_ref, sem = refs[9:] if aliased else refs[8:]
    slot = 0 if aliased else mixer
    n = bb * tt
    n_seg = n // chunk
    assert bb == 1 or tt == chunk
    assert chunk & (chunk - 1) == 0
    shared_state = bb == 1
    i = pl.program_id(1)

    @pl.when((pl.program_id(0) == 0) & (i == 0))
    def _():
        _stream_cast(win_hbm.at[mixer], win_ref, stage_ref, sem, stage_ref.shape[1])

    @pl.when(i == 0)
    def _():
        for other in range(s_ref.shape[0]):
            if other != slot:
                s_ref[other] = jnp.zeros(s_ref.shape[1:], F32)
        s_ref[slot] = s0_ref[0]

    x = x_ref[...].reshape(n, D_MODEL)
    xn = _rmsnorm(x, g_ref[0]).astype(BF16)
    proj = jnp.dot(xn, win_ref[:, :_GLA_MAIN], preferred_element_type=F32)
    q3 = (proj[:, :GLA_DK] * (GLA_HK ** -0.5)).reshape(n_seg, chunk, GLA_DK)
    k3 = proj[:, GLA_DK:2 * GLA_DK].reshape(n_seg, chunk, GLA_DK)
    v = proj[:, 2 * GLA_DK:2 * GLA_DK + GLA_DV]
    gsilu = jax.nn.silu(proj[:, 2 * GLA_DK + GLA_DV:])
    a_lo = jnp.dot(xn, win_ref[:, _GLA_MAIN:], preferred_element_type=F32)
    z = _mm(a_lo, wa2_ref[0]) + ba_ref[0]
    nz = -z
    softplus_nz = jnp.maximum(nz, 0.0) + jnp.log(1.0 + jnp.exp(-jnp.abs(nz)))
    la3 = (softplus_nz * (-LOG2_E / GLA_TAU)).reshape(n_seg, chunk, GLA_DK)

    early_cast = (lambda t: t.astype(BF16)) if chunk % (2 * SUBLANES) == 0 else (lambda t: t)
    v = early_cast(v)
    t_idx = lax.broadcasted_iota(jnp.int32, (n_seg, chunk, GLA_HK), 1)
    rg = min(n, MXU_DIM)
    row = lax.broadcasted_iota(jnp.int32, (rg, rg), 0)
    col = lax.broadcasted_iota(jnp.int32, (rg, rg), 1)
    same_chunk_causal = (col <= row) & (col >= (row & (-chunk)))

    for h in range(GLA_HEADS):
        kc = slice(h * GLA_HK, (h + 1) * GLA_HK)
        vc = slice(h * GLA_HV, (h + 1) * GLA_HV)
        bcum = la3[:, :, kc]
        s = 1
        while s < chunk:
            bcum = bcum + jnp.where(t_idx >= s, pltpu.roll(bcum, s, 1), 0.0)
            s *= 2
        gl = bcum[:, chunk - 1:chunk, :]
        q_in = early_cast((q3[:, :, kc] * jnp.exp2(bcum)).reshape(n, GLA_HK))
        k_dec = k3[:, :, kc] * jnp.exp2(-bcum)
        k_in = early_cast(k_dec.reshape(n, GLA_HK))
        k_end = early_cast((k_dec * jnp.exp2(gl)).reshape(n, GLA_HK))

        for r in range(n // rg):
            rows = slice(r * rg, (r + 1) * rg)
            att = lax.dot_general(q_in[rows].astype(BF16), k_in[rows].astype(BF16),
                                  (((1,), (1,)), ((), ())), preferred_element_type=F32)
            att = jnp.where(same_chunk_causal, att, 0.0).astype(BF16)
            oc_ref[rows, vc] = jnp.dot(att, v[rows, vc].astype(BF16), preferred_element_type=F32)

        state = None
        for c in range(n_seg):
            rows = slice(c * chunk, (c + 1) * chunk)
            sb = 0 if shared_state else c
            if state is None or not shared_state:
                state = s_ref[slot, sb, h]
            oc_ref[rows, vc] = oc_ref[rows, vc] + jnp.dot(
                q_in[rows].astype(BF16), state.astype(BF16), preferred_element_type=F32)
            eg = jnp.exp2(jnp.broadcast_to(gl[c], (GLA_HK, GLA_HK))).T
            eg = jnp.concatenate([eg] * (GLA_HV // GLA_HK), axis=1)
            state = eg * state + lax.dot_general(
                k_end[rows].astype(BF16), v[rows, vc].astype(BF16),
                (((0,), (0,)), ((), ())), preferred_element_type=F32)
            if not shared_state or c == n_seg - 1:
                s_ref[slot, sb, h] = state

    o = oc_ref[...]
    heads = []
    for h in range(GLA_HEADS):
        heads.append(_rmsnorm(o[:, h * GLA_HV:(h + 1) * GLA_HV], ng_ref[0]))
    on = jnp.concatenate(heads, axis=1)
    out = x + _mm(on * gsilu, wout_ref[0])
    o_ref[...] = out.reshape(o_ref.shape)


def _gla_block(x, group, s0, s_new, w, layer, bb, tt, chunk):
    n = bb * tt
    j = layer // 2
    tile = group.tile(bb, tt)
    state = pl.BlockSpec((1, bb, GLA_HEADS, GLA_HK, GLA_HV), lambda b, i: (j, b, 0, 0, 0))
    gla_in = _GLA_MAIN + GLA_RANK
    in_specs = [tile, state, _layer_spec((1, D_MODEL), layer),
                pl.BlockSpec(memory_space=pl.ANY), _layer_spec((GLA_RANK, GLA_DK), j),
                _layer_spec((1, GLA_DK), j), _layer_spec((1, GLA_HV), j),
                _layer_spec((GLA_DV, D_MODEL), j)]
    args = [x, s0, w["norm_mix_g"], w["gla_w_in"], w["gla_w_a2"], w["gla_b_a"],
            w["gla_norm_g"], w["gla_w_out"]]
    if s_new is None:
        aliases = {}
        state_out = pl.BlockSpec((s0.shape[0], bb, GLA_HEADS, GLA_HK, GLA_HV),
                                 lambda b, i: (0, b, 0, 0, 0))
    else:
        in_specs.append(pl.BlockSpec(memory_space=pl.ANY))
        args.append(s_new)
        aliases = {len(args) - 1: 1}
        state_out = state
    return pl.pallas_call(
        functools.partial(_gla_kernel, bb=bb, tt=tt, chunk=chunk, mixer=j,
                          aliased=s_new is not None),
        grid=group.grid(bb, tt),
        in_specs=in_specs,
        out_specs=[tile, state_out],
        out_shape=[jax.ShapeDtypeStruct(group.shape, F32), jax.ShapeDtypeStruct(s0.shape, F32)],
        input_output_aliases=aliases,
        scratch_shapes=[pltpu.VMEM((n, GLA_DV), F32), pltpu.VMEM((D_MODEL, gla_in), BF16),
                        pltpu.VMEM((2, LANES, gla_in), F32), pltpu.SemaphoreType.DMA((2,))],
        compiler_params=_params(2),
        name="gla_block",
    )(*args)


class _Stream(NamedTuple):
    x: jax.Array
    group: _Group
    mem_k: jax.Array
    mem_v: jax.Array
    rg_h: jax.Array
    rg_conv: jax.Array
    gla_s: jax.Array
    tiles: dict


def _run_layers(streams, w):
    xs = [s.x for s in streams]
    hs = [[] for _ in streams]
    convs = [[] for _ in streams]
    s_new = [None for _ in streams]
    for layer in range(DEPTH):
        for n, s in enumerate(streams):
            if layer % 2 == 0:
                xs[n], cb, hl = _rg_block(xs[n], s.group, s.rg_conv, s.rg_h, w, layer,
                                          *s.tiles["rg"])
                convs[n].append(cb)
                hs[n].append(hl)
            else:
                gla_tile = s.tiles["gla"] if s_new[n] is None else s.tiles["gla_in_place"]
                xs[n], s_new[n] = _gla_block(xs[n], s.group, s.gla_s, s_new[n], w, layer,
                                             *gla_tile)
            xs[n] = _xattn(xs[n], s.group, w, layer, s.mem_k, s.mem_v, *s.tiles["xattn"])
        a, b = streams
        xs = list(_mlp(xs[0], a.group, a.tiles["mlp"], xs[1], b.group, b.tiles["mlp"], w, layer))
    results = []
    for n, s in enumerate(streams):
        h_all = jnp.concatenate(hs[n], axis=0).reshape(len(hs[n]), s.group.nb, D_RNN)
        results.append((xs[n], h_all, jnp.concatenate(convs[n], axis=0), s_new[n]))
    return results


def kernel(x_prompt, x_sample, mem_prompt, state_rglru_h, state_rglru_conv, state_gla_S, cache_mem_k, cache_mem_v, norm_mix_g, norm_xa_g, norm_mem_g, norm_mlp_g, final_norm_g, rg_w_in, rg_conv_w, rg_conv_b, rg_w_a, rg_b_a, rg_w_x, rg_b_x, rg_lambda, rg_w_out, gla_w_in, gla_w_a2, gla_b_a, gla_norm_g, gla_w_out, xa_wq, xa_wk, xa_wv, xa_wo, mlp_w1, mlp_w2):
    batch, seq, _ = x_prompt.shape
    dec_batch, dec_seq, _ = x_sample.shape
    n_a = rg_w_in.shape[0]
    n_b = gla_w_in.shape[0]
    assert batch == SUBLANES and dec_seq == SUBLANES

    def rows(p):
        return p.reshape(p.shape[0], 1, p.shape[1])

    gates = jnp.concatenate(
        [jax.vmap(_block_diag_tiles)(rg_w_a), jax.vmap(_block_diag_tiles)(rg_w_x)], axis=-1)
    w = dict(
        norm_mix_g=rows(norm_mix_g), norm_xa_g=rows(norm_xa_g), norm_mlp_g=rows(norm_mlp_g),
        final_norm_g=final_norm_g.reshape(1, D_MODEL),
        rg_w_in=rg_w_in.astype(BF16), rg_conv_w=rg_conv_w, rg_conv_b=rows(rg_conv_b),
        rg_w_gates=gates.astype(BF16), rg_b_a=rows(rg_b_a), rg_b_x=rows(rg_b_x),
        rg_lambda=rows(rg_lambda), rg_w_out=rg_w_out.astype(BF16),
        gla_w_in=gla_w_in, gla_w_a2=gla_w_a2.astype(BF16),
        gla_b_a=rows(gla_b_a), gla_norm_g=rows(gla_norm_g), gla_w_out=gla_w_out.astype(BF16),
        xa_wq=xa_wq.astype(BF16), xa_wo=xa_wo.astype(BF16),
        mlp_w1=mlp_w1, mlp_w2=mlp_w2,
    )

    mem2d = mem_prompt.reshape(batch * N_MEM, D_MODEL)
    mem_k_prompt, mem_v_prompt, mkb, mvb = _mem_kv(
        mem2d, rows(norm_mem_g), xa_wk.astype(BF16), xa_wv.astype(BF16))
    prompt_tiles = dict(rg=(SUBLANES, ROW_TILE // SUBLANES), gla=(1, ROW_TILE, GLA_CHUNK),
                        gla_in_place=(1, ROW_TILE, GLA_CHUNK), xattn=(1, ROW_TILE),
                        mlp=(1, ROW_TILE))
    prompt = _Stream(
        x_prompt, _Group(batch, seq),
        mkb.reshape(DEPTH, batch, N_MEM, D_MODEL), mvb.reshape(DEPTH, batch, N_MEM, D_MODEL),
        jnp.zeros((n_a, batch, 1, D_RNN), F32), jnp.zeros((n_a, batch, CONV_W - 1, D_RNN), F32),
        jnp.zeros((n_b, batch, GLA_HEADS, GLA_HK, GLA_HV), F32), prompt_tiles)

    sample_chunk = GLA_CHUNK if dec_seq % GLA_CHUNK == 0 else dec_seq
    sample_tiles = dict(rg=(32, dec_seq), gla=(8, dec_seq, sample_chunk),
                        gla_in_place=(16, dec_seq, sample_chunk), xattn=(4, dec_seq),
                        mlp=(ROW_TILE // dec_seq, dec_seq))
    sample = _Stream(
        x_sample, _Group(dec_batch, dec_seq), cache_mem_k, cache_mem_v,
        state_rglru_h.reshape(n_a, dec_batch, 1, D_RNN), state_rglru_conv, state_gla_S,
        sample_tiles)

    (y_p, h_p, conv_p, s_p), (y_s, h_s, conv_s, s_s) = _run_layers((prompt, sample), w)
    return (y_p, y_s, mem_k_prompt, mem_v_prompt, h_p, conv_p, s_p, h_s, conv_s, s_s)
```

```python
import functools
from typing import NamedTuple

import jax
import jax.numpy as jnp
from jax import lax
from jax.experimental import pallas as pl
from jax.experimental.pallas import tpu as pltpu

F32 = jnp.float32
BF16 = jnp.bfloat16

D_MODEL = 1024
DEPTH = 4
D_RNN = D_MODEL
RG_BLOCKS = 16
RG_BW = D_RNN // RG_BLOCKS
CONV_W = 4
RG_C = 8.0
GLA_HEADS = 4
GLA_DK = D_MODEL // 2
GLA_DV = D_MODEL
GLA_HK = GLA_DK // GLA_HEADS
GLA_HV = GLA_DV // GLA_HEADS
GLA_RANK = 16
GLA_TAU = 16.0
GLA_CHUNK = 64
N_MEM = 256
XA_HEADS = 4
XA_HD = D_MODEL // XA_HEADS
D_FF = 4 * D_MODEL
EPS = 1e-6

SUBLANES = 8
LANES = 128
MXU_DIM = 256
VMEM_LIMIT_BYTES = 56 * 1024 * 1024
MASKED_SCORE = -1e30
LOG2_E = 1.4426950408889634
ROW_TILE = 512
MLP_STAGE_ELEMS = 512 * 1024


class _Group(NamedTuple):
    nb: int
    t: int

    @property
    def shape(self):
        return (self.nb, self.t, D_MODEL)

    def tile(self, bb, tt):
        return pl.BlockSpec((bb, tt, D_MODEL), lambda b, i: (b, i, 0))

    def grid(self, bb, tt):
        return (self.nb // bb, self.t // tt)


def _params(n_axes):
    return pltpu.CompilerParams(
        dimension_semantics=("arbitrary",) * n_axes,
        vmem_limit_bytes=VMEM_LIMIT_BYTES,
    )


def _layer_spec(shape, layer):
    zeros = (0,) * len(shape)
    return pl.BlockSpec((1,) + tuple(shape), lambda *_: (layer,) + zeros,
                        pipeline_mode=pl.Buffered(1))


def _rmsnorm(x, g):
    ms = jnp.mean(x * x, axis=-1, keepdims=True)
    return x * lax.rsqrt(ms + EPS) * g


def _mm(a, w):
    return jnp.dot(a.astype(BF16), w, preferred_element_type=F32)


def _softplus(z):
    return jnp.maximum(z, 0.0) + jnp.log1p(jnp.exp(-jnp.abs(z)))


def _memkv_kernel(mem_ref, g_ref, wk_ref, wv_ref, k_ref, v_ref, kb_ref, vb_ref, *, nbm):
    mn = _rmsnorm(mem_ref[...], g_ref[0]).astype(BF16)
    k = jnp.dot(mn, wk_ref[0], preferred_element_type=F32)
    v = jnp.dot(mn, wv_ref[0], preferred_element_type=F32)
    k_ref[0] = k.reshape(nbm, N_MEM, XA_HEADS, XA_HD)
    v_ref[0] = v.reshape(nbm, N_MEM, XA_HEADS, XA_HD)
    kb_ref[0] = k.astype(BF16)
    vb_ref[0] = v.astype(BF16)


def _mem_kv(mem2d, norm_g, wk, wv, nbm=2):
    n = mem2d.shape[0]
    nb = n // N_MEM
    tm = nbm * N_MEM
    row = pl.BlockSpec((tm, D_MODEL), lambda l, i: (i, 0))
    per_layer = lambda shape: pl.BlockSpec((1,) + shape, lambda l, i: (l, 0, 0))
    out5 = pl.BlockSpec((1, nbm, N_MEM, XA_HEADS, XA_HD), lambda l, i: (l, i, 0, 0, 0))
    out3 = pl.BlockSpec((1, tm, D_MODEL), lambda l, i: (l, i, 0))
    return pl.pallas_call(
        functools.partial(_memkv_kernel, nbm=nbm),
        grid=(DEPTH, n // tm),
        in_specs=[row, per_layer((1, D_MODEL)), per_layer((D_MODEL, D_MODEL)),
                  per_layer((D_MODEL, D_MODEL))],
        out_specs=[out5, out5, out3, out3],
        out_shape=[jax.ShapeDtypeStruct((DEPTH, nb, N_MEM, XA_HEADS, XA_HD), F32)] * 2
        + [jax.ShapeDtypeStruct((DEPTH, n, D_MODEL), BF16)] * 2,
        compiler_params=_params(2),
        name="mem_kv",
    )(mem2d, norm_g, wk, wv)


def _stream_cast(src, dst, stage, sem, rows):
    n_chunks = src.shape[0] // rows

    def copy(k):
        return pltpu.make_async_copy(src.at[pl.ds(k * rows, rows)], stage.at[k % 2], sem.at[k % 2])

    copy(0).start()
    for k in range(n_chunks):
        if k + 1 < n_chunks:
            copy(k + 1).start()
        copy(k).wait()
        dst[k * rows:(k + 1) * rows, :] = stage[k % 2].astype(BF16)


def _mlp_kernel(xa_ref, xb_ref, g_ref, w1_hbm, w2_hbm, gf_ref, oa_ref, ob_ref,
                w1_ref, w2_ref, stage1_ref, stage2_ref, sem1, sem2, *,
                layer, steps_a, final, f_chunk):
    p = pl.program_id(0)

    @pl.when(p == 0)
    def _():
        _stream_cast(w1_hbm.at[layer], w1_ref, stage1_ref, sem1, stage1_ref.shape[1])
        _stream_cast(w2_hbm.at[layer], w2_ref, stage2_ref, sem2, stage2_ref.shape[1])

    def apply(x_ref, o_ref):
        x = x_ref[...].reshape(-1, D_MODEL)
        xn = _rmsnorm(x, g_ref[0]).astype(BF16)
        acc = x
        for c in range(D_FF // f_chunk):
            cols = slice(c * f_chunk, (c + 1) * f_chunk)
            hid = jnp.dot(xn, w1_ref[:, cols], preferred_element_type=F32)
            act = jnp.square(jnp.maximum(hid, 0.0)).astype(BF16)
            acc = acc + jnp.dot(act, w2_ref[cols, :], preferred_element_type=F32)
        if final:
            acc = _rmsnorm(acc, gf_ref[...])
        o_ref[...] = acc.reshape(o_ref.shape)

    pl.when(p < steps_a)(lambda: apply(xa_ref, oa_ref))
    pl.when(p >= steps_a)(lambda: apply(xb_ref, ob_ref))


def _mlp(xa, group_a, tile_a, xb, group_b, tile_b, w, layer, f_chunk=1024):
    final = layer == DEPTH - 1
    t_steps_a = group_a.grid(*tile_a)[1]
    t_steps_b = group_b.grid(*tile_b)[1]
    steps_a = group_a.grid(*tile_a)[0] * t_steps_a
    steps_b = group_b.grid(*tile_b)[0] * t_steps_b

    def index_a(p):
        q = jnp.minimum(p, steps_a - 1)
        return (q // t_steps_a, q % t_steps_a, 0)

    def index_b(p):
        q = jnp.maximum(p - steps_a, 0)
        return (q // t_steps_b, q % t_steps_b, 0)

    spec_a = pl.BlockSpec(tuple(tile_a) + (D_MODEL,), index_a)
    spec_b = pl.BlockSpec(tuple(tile_b) + (D_MODEL,), index_b)
    hbm = pl.BlockSpec(memory_space=pl.ANY)
    return pl.pallas_call(
        functools.partial(_mlp_kernel, layer=layer, steps_a=steps_a, final=final, f_chunk=f_chunk),
        grid=(steps_a + steps_b,),
        in_specs=[spec_a, spec_b, _layer_spec((1, D_MODEL), layer), hbm, hbm,
                  pl.BlockSpec((1, D_MODEL), lambda p: (0, 0))],
        out_specs=[spec_a, spec_b],
        out_shape=[jax.ShapeDtypeStruct(group_a.shape, F32),
                   jax.ShapeDtypeStruct(group_b.shape, F32)],
        scratch_shapes=[pltpu.VMEM((D_MODEL, D_FF), BF16), pltpu.VMEM((D_FF, D_MODEL), BF16),
                        pltpu.VMEM((2, MLP_STAGE_ELEMS // D_FF, D_FF), F32),
                        pltpu.VMEM((2, MLP_STAGE_ELEMS // D_MODEL, D_MODEL), F32),
                        pltpu.SemaphoreType.DMA((2,)), pltpu.SemaphoreType.DMA((2,))],
        compiler_params=_params(1),
        name="mlp",
    )(xa, xb, w["norm_mlp_g"], w["mlp_w1"], w["mlp_w2"], w["final_norm_g"])


def _xattn_kernel(x_ref, g_ref, wq_ref, wo_ref, k_ref, v_ref, o_ref, att_ref, *, bb, tt, packed_heads):
    n = bb * tt
    x = x_ref[...].reshape(n, D_MODEL)
    xn = _rmsnorm(x, g_ref[0])
    q = _mm(xn, wq_ref[0]) * (XA_HD ** -0.5 * LOG2_E)

    def softmax(s):
        e = jnp.exp2(s - jnp.max(s, axis=-1, keepdims=True))
        return (e / jnp.sum(e, axis=-1, keepdims=True)).astype(BF16)

    nt_dims = (((1,), (1,)), ((), ()))
    if packed_heads:
        rows_b = XA_HEADS * tt
        shape = (bb * rows_b, N_MEM * XA_HEADS)
        q_head = (lax.broadcasted_iota(jnp.int32, shape, 0) % rows_b) // tt
        kv_head = lax.broadcasted_iota(jnp.int32, shape, 1) % XA_HEADS
        same_head = q_head == kv_head
        scores = []
        for b in range(bb):
            qs = jnp.concatenate(
                [q[b * tt:(b + 1) * tt, h * XA_HD:(h + 1) * XA_HD] for h in range(XA_HEADS)],
                axis=0).astype(BF16)
            kf = k_ref[0, b].reshape(N_MEM * XA_HEADS, XA_HD).astype(BF16)
            scores.append(lax.dot_general(qs, kf, nt_dims, preferred_element_type=F32))
        s = jnp.concatenate(scores, axis=0)
        p = softmax(jnp.where(same_head, s, MASKED_SCORE))
        for b in range(bb):
            vf = v_ref[0, b].reshape(N_MEM * XA_HEADS, XA_HD).astype(BF16)
            o = jnp.dot(p[b * rows_b:(b + 1) * rows_b], vf, preferred_element_type=F32)
            for h in range(XA_HEADS):
                att_ref[b * tt:(b + 1) * tt, h * XA_HD:(h + 1) * XA_HD] = o[h * tt:(h + 1) * tt]
    else:
        qb = q.astype(BF16)
        for b in range(bb):
            rows = slice(b * tt, (b + 1) * tt)
            head_cols = [slice(h * XA_HD, (h + 1) * XA_HD) for h in range(XA_HEADS)]
            s = jnp.concatenate(
                [lax.dot_general(qb[rows, cols], k_ref[0, b, :, cols], nt_dims,
                                 preferred_element_type=F32) for cols in head_cols], axis=0)
            p = softmax(s)
            for h, cols in enumerate(head_cols):
                att_ref[rows, cols] = jnp.dot(
                    p[h * tt:(h + 1) * tt], v_ref[0, b, :, cols], preferred_element_type=F32)
    out = x + _mm(att_ref[...], wo_ref[0])
    o_ref[...] = out.reshape(o_ref.shape)


def _xattn(x, group, w, layer, k, v, bb, tt):
    packed_heads = k.ndim == 5
    if packed_heads:
        kv = pl.BlockSpec((1, bb, N_MEM, XA_HEADS, XA_HD), lambda b, i: (layer, b, 0, 0, 0))
    else:
        kv = pl.BlockSpec((1, bb, N_MEM, D_MODEL), lambda b, i: (layer, b, 0, 0))
    return pl.pallas_call(
        functools.partial(_xattn_kernel, bb=bb, tt=tt, packed_heads=packed_heads),
        grid=group.grid(bb, tt),
        in_specs=[group.tile(bb, tt), _layer_spec((1, D_MODEL), layer),
                  _layer_spec((D_MODEL, D_MODEL), layer), _layer_spec((D_MODEL, D_MODEL), layer),
                  kv, kv],
        out_specs=group.tile(bb, tt),
        out_shape=jax.ShapeDtypeStruct(group.shape, F32),
        scratch_shapes=[pltpu.VMEM((bb * tt, D_MODEL), F32)],
        compiler_params=_params(2),
        name="xattn",
    )(x, w["norm_xa_g"], w["xa_wq"], w["xa_wo"], k, v)


def _rg_kernel(x_ref, buf_ref, h0_ref, g_ref, win_ref, cw_ref, cb_ref, wg_ref, ba_ref, bx_ref,
               lam_ref, wout_ref, o_ref, nbuf_ref, hl_ref, u_ref, hc_ref, tm_ref, *, bb, tt, slabs):
    n = bb * tt
    pad = SUBLANES
    n_prev = CONV_W - 1
    assert (bb == SUBLANES) if slabs else (tt == SUBLANES)
    i = pl.program_id(1)

    @pl.when(i == 0)
    def _():
        u_ref[...] = jnp.zeros((bb, pad, D_RNN), F32)
        if slabs:
            for j in range(n_prev):
                u_ref[pad - n_prev + j] = buf_ref[0, :, j, :]
        else:
            u_ref[:, pad - n_prev:pad, :] = buf_ref[0]
        hc_ref[...] = h0_ref[0]

    if slabs:
        x = jnp.concatenate([x_ref[:, t, :] for t in range(tt)], axis=0)
    else:
        x = x_ref[...].reshape(n, D_MODEL)
    xn = _rmsnorm(x, g_ref[0])
    yx = _mm(xn, win_ref[0])
    gate = jax.nn.gelu(yx[:, :D_RNN])
    u = yx[:, D_RNN:]

    def tap(j):
        return cw_ref[0, CONV_W - 1 - j:CONV_W - j, :]

    if slabs:
        steps = [u_ref[pad - n_prev + j] for j in range(n_prev)]
        steps += [u[t * bb:(t + 1) * bb] for t in range(tt)]
        xc_steps = []
        for t in range(tt):
            acc = cb_ref[0] + tap(0) * steps[n_prev + t]
            for j in range(1, CONV_W):
                acc = acc + tap(j) * steps[n_prev + t - j]
            xc_steps.append(acc)
        xc2 = jnp.concatenate(xc_steps, axis=0)
        for j in range(n_prev):
            u_ref[pad - n_prev + j] = steps[tt + j]
            nbuf_ref[0, :, j, :] = steps[tt + j]
    else:
        u3 = u.reshape(bb, tt, D_RNN)
        ext = jnp.concatenate([u_ref[...], u3], axis=1)
        xc = cb_ref[0].reshape(1, 1, D_RNN) + tap(0).reshape(1, 1, D_RNN) * u3
        for j in range(1, CONV_W):
            xc = xc + tap(j).reshape(1, 1, D_RNN) * pltpu.roll(ext, j, 1)[:, pad:, :]
        nbuf_ref[0] = ext[:, pad + tt - n_prev:, :]
        u_ref[...] = ext[:, tt:, :]
        xc2 = xc.reshape(n, D_RNN)

    gates = [_mm(xc2[:, c * MXU_DIM:(c + 1) * MXU_DIM], wg_ref[0, c]) for c in range(D_RNN // MXU_DIM)]
    r = jax.nn.sigmoid(jnp.concatenate([gc[:, :MXU_DIM] for gc in gates], axis=1) + ba_ref[0])
    ig = jax.nn.sigmoid(jnp.concatenate([gc[:, MXU_DIM:] for gc in gates], axis=1) + bx_ref[0])
    log_a = r * ((-RG_C) * _softplus(-lam_ref[0]))
    a = jnp.exp(log_a)
    m2 = -jnp.tanh(log_a) * (a * a + 1.0)
    mult = jnp.where(m2 > 0.0, m2 * lax.rsqrt(m2), 0.0)
    bt = mult * (ig * xc2)

    if slabs:
        carry = hc_ref[:, 0, :]
        h_steps = []
        for t in range(tt):
            rows = slice(t * bb, (t + 1) * bb)
            carry = a[rows] * carry + bt[rows]
            h_steps.append(carry)
        h = jnp.concatenate(h_steps, axis=0)
        hc_ref[:, 0, :] = carry
        hl_ref[0, :, 0, :] = carry
    else:
        ag = a.reshape(bb, tt, D_RNN)
        bg = bt.reshape(bb, tt, D_RNN)
        t_idx = lax.broadcasted_iota(jnp.int32, (bb, tt, D_RNN), 1)
        s = 1
        while s < tt:
            keep = t_idx >= s
            a_s = jnp.where(keep, pltpu.roll(ag, s, 1), 1.0)
            b_s = jnp.where(keep, pltpu.roll(bg, s, 1), 0.0)
            bg = ag * b_s + bg
            ag = ag * a_s
            s *= 2
        h3 = ag * hc_ref[...] + bg
        h = h3.reshape(n, D_RNN)
        hc_ref[...] = h3[:, tt - 1:tt, :]
        hl_ref[0] = h3[:, tt - 1:tt, :]

    out = x + _mm(gate * h, wout_ref[0])
    if slabs:
        tm_ref[...] = out.reshape(tt, bb, D_MODEL)
        for b in range(bb):
            o_ref[b] = tm_ref[:, b, :]
    else:
        o_ref[...] = out.reshape(o_ref.shape)


def _rg_block(x, group, conv_buf, h0, w, layer, bb, tt):
    nb, t = group.nb, group.t
    j = layer // 2
    slabs = bb == SUBLANES and tt > SUBLANES
    tile = group.tile(bb, tt)
    buf_in = pl.BlockSpec((1, bb, CONV_W - 1, D_RNN), lambda b, i: (j, b, 0, 0))
    h_in = pl.BlockSpec((1, bb, 1, D_RNN), lambda b, i: (j, b, 0, 0))
    buf_out = pl.BlockSpec((1, bb, CONV_W - 1, D_RNN), lambda b, i: (0, b, 0, 0))
    h_out = pl.BlockSpec((1, bb, 1, D_RNN), lambda b, i: (0, b, 0, 0))
    vec = _layer_spec((1, D_RNN), j)
    return pl.pallas_call(
        functools.partial(_rg_kernel, bb=bb, tt=tt, slabs=slabs),
        grid=group.grid(bb, tt),
        in_specs=[tile, buf_in, h_in, _layer_spec((1, D_MODEL), layer),
                  _layer_spec((D_MODEL, 2 * D_RNN), j), _layer_spec((CONV_W, D_RNN), j), vec,
                  _layer_spec((D_RNN // MXU_DIM, MXU_DIM, 2 * MXU_DIM), j), vec, vec, vec,
                  _layer_spec((D_RNN, D_MODEL), j)],
        out_specs=[tile, buf_out, h_out],
        out_shape=[jax.ShapeDtypeStruct(group.shape, F32),
                   jax.ShapeDtypeStruct((1, nb, CONV_W - 1, D_RNN), F32),
                   jax.ShapeDtypeStruct((1, nb, 1, D_RNN), F32)],
        scratch_shapes=[pltpu.VMEM((bb, SUBLANES, D_RNN), F32),
                        pltpu.VMEM((bb, 1, D_RNN), F32),
                        pltpu.VMEM((tt, bb, D_MODEL) if slabs else (1, SUBLANES, LANES), F32)],
        compiler_params=_params(2),
        name="rg_block",
    )(x, conv_buf, h0, w["norm_mix_g"], w["rg_w_in"], w["rg_conv_w"], w["rg_conv_b"],
      w["rg_w_gates"], w["rg_b_a"], w["rg_b_x"], w["rg_lambda"], w["rg_w_out"])


def _block_diag_tiles(w):
    per = MXU_DIM // RG_BW
    w4 = w.reshape(RG_BLOCKS // per, per, RG_BW, RG_BW)
    eye = jnp.eye(per, dtype=w.dtype)
    t = w4[:, :, :, None, :] * eye[None, :, None, :, None]
    return t.reshape(RG_BLOCKS // per, MXU_DIM, MXU_DIM)


_GLA_MAIN = 2 * GLA_DK + 2 * GLA_DV


def _gla_kernel(*refs, bb, tt, chunk, slot, aliased):
    (x_ref, s0_ref, g_ref, win_ref, wa2_ref, ba_ref, ng_ref, wout_ref) = refs[:8]
    o_ref, s_ref, oc_ref = refs[9:] if aliased else refs[8:]
    if aliased:
        slot = 0
    n = bb * tt
    n_seg = n // chunk
    assert bb == 1 or tt == chunk
    assert chunk & (chunk - 1) == 0
    shared_state = bb == 1
    i = pl.program_id(1)

    @pl.when(i == 0)
    def _():
        for other in range(s_ref.shape[0]):
            if other != slot:
                s_ref[other] = jnp.zeros(s_ref.shape[1:], F32)
        s_ref[slot] = s0_ref[0]

    x = x_ref[...].reshape(n, D_MODEL)
    xn = _rmsnorm(x, g_ref[0]).astype(BF16)
    proj = jnp.dot(xn, win_ref[0, :, :_GLA_MAIN], preferred_element_type=F32)
    q3 = (proj[:, :GLA_DK] * (GLA_HK ** -0.5)).reshape(n_seg, chunk, GLA_DK)
    k3 = proj[:, GLA_DK:2 * GLA_DK].reshape(n_seg, chunk, GLA_DK)
    v = proj[:, 2 * GLA_DK:2 * GLA_DK + GLA_DV]
    gsilu = jax.nn.silu(proj[:, 2 * GLA_DK + GLA_DV:])
    a_lo = jnp.dot(xn, win_ref[0, :, _GLA_MAIN:], preferred_element_type=F32)
    z = _mm(a_lo, wa2_ref[0]) + ba_ref[0]
    nz = -z
    softplus_nz = jnp.maximum(nz, 0.0) + jnp.log(1.0 + jnp.exp(-jnp.abs(nz)))
    la3 = (softplus_nz * (-LOG2_E / GLA_TAU)).reshape(n_seg, chunk, GLA_DK)

    early_cast = (lambda t: t.astype(BF16)) if chunk % (2 * SUBLANES) == 0 else (lambda t: t)
    v = early_cast(v)
    t_idx = lax.broadcasted_iota(jnp.int32, (n_seg, chunk, GLA_HK), 1)
    rg = min(n, MXU_DIM)
    row = lax.broadcasted_iota(jnp.int32, (rg, rg), 0)
    col = lax.broadcasted_iota(jnp.int32, (rg, rg), 1)
    same_chunk_causal = (col <= row) & (col >= (row & (-chunk)))

    for h in range(GLA_HEADS):
        kc = slice(h * GLA_HK, (h + 1) * GLA_HK)
        vc = slice(h * GLA_HV, (h + 1) * GLA_HV)
        bcum = la3[:, :, kc]
        s = 1
        while s < chunk:
            bcum = bcum + jnp.where(t_idx >= s, pltpu.roll(bcum, s, 1), 0.0)
            s *= 2
        gl = bcum[:, chunk - 1:chunk, :]
        q_in = early_cast((q3[:, :, kc] * jnp.exp2(bcum)).reshape(n, GLA_HK))
        k_dec = k3[:, :, kc] * jnp.exp2(-bcum)
        k_in = early_cast(k_dec.reshape(n, GLA_HK))
        k_end = early_cast((k_dec * jnp.exp2(gl)).reshape(n, GLA_HK))

        for r in range(n // rg):
            rows = slice(r * rg, (r + 1) * rg)
            att = lax.dot_general(q_in[rows].astype(BF16), k_in[rows].astype(BF16),
                                  (((1,), (1,)), ((), ())), preferred_element_type=F32)
            att = jnp.where(same_chunk_causal, att, 0.0).astype(BF16)
            oc_ref[rows, vc] = jnp.dot(att, v[rows, vc].astype(BF16), preferred_element_type=F32)

        state = None
        for c in range(n_seg):
            rows = slice(c * chunk, (c + 1) * chunk)
            sb = 0 if shared_state else c
            if state is None or not shared_state:
                state = s_ref[slot, sb, h]
            oc_ref[rows, vc] = oc_ref[rows, vc] + jnp.dot(
                q_in[rows].astype(BF16), state.astype(BF16), preferred_element_type=F32)
            eg = jnp.exp2(jnp.broadcast_to(gl[c], (GLA_HK, GLA_HK))).T
            eg = jnp.concatenate([eg] * (GLA_HV // GLA_HK), axis=1)
            state = eg * state + lax.dot_general(
                k_end[rows].astype(BF16), v[rows, vc].astype(BF16),
                (((0,), (0,)), ((), ())), preferred_element_type=F32)
            if not shared_state or c == n_seg - 1:
                s_ref[slot, sb, h] = state

    o = oc_ref[...]
    heads = []
    for h in range(GLA_HEADS):
        heads.append(_rmsnorm(o[:, h * GLA_HV:(h + 1) * GLA_HV], ng_ref[0]))
    on = jnp.concatenate(heads, axis=1)
    out = x + _mm(on * gsilu, wout_ref[0])
    o_ref[...] = out.reshape(o_ref.shape)


def _gla_block(x, group, s0, s_new, w, layer, bb, tt, chunk):
    n = bb * tt
    j = layer // 2
    tile = group.tile(bb, tt)
    state = pl.BlockSpec((1, bb, GLA_HEADS, GLA_HK, GLA_HV), lambda b, i: (j, b, 0, 0, 0))
    in_specs = [tile, state, _layer_spec((1, D_MODEL), layer),
                _layer_spec((D_MODEL, _GLA_MAIN + GLA_RANK), j), _layer_spec((GLA_RANK, GLA_DK), j),
                _layer_spec((1, GLA_DK), j), _layer_spec((1, GLA_HV), j),
                _layer_spec((GLA_DV, D_MODEL), j)]
    args = [x, s0, w["norm_mix_g"], w["gla_w_in"], w["gla_w_a2"], w["gla_b_a"],
            w["gla_norm_g"], w["gla_w_out"]]
    if s_new is None:
        aliases = {}
        state_out = pl.BlockSpec((s0.shape[0], bb, GLA_HEADS, GLA_HK, GLA_HV),
                                 lambda b, i: (0, b, 0, 0, 0))
    else:
        in_specs.append(pl.BlockSpec(memory_space=pl.ANY))
        args.append(s_new)
        aliases = {len(args) - 1: 1}
        state_out = state
    return pl.pallas_call(
        functools.partial(_gla_kernel, bb=bb, tt=tt, chunk=chunk, slot=j,
                          aliased=s_new is not None),
        grid=group.grid(bb, tt),
        in_specs=in_specs,
        out_specs=[tile, state_out],
        out_shape=[jax.ShapeDtypeStruct(group.shape, F32), jax.ShapeDtypeStruct(s0.shape, F32)],
        input_output_aliases=aliases,
        scratch_shapes=[pltpu.VMEM((n, GLA_DV), F32)],
        compiler_params=_params(2),
        name="gla_block",
    )(*args)


class _Stream(NamedTuple):
    x: jax.Array
    group: _Group
    mem_k: jax.Array
    mem_v: jax.Array
    rg_h: jax.Array
    rg_conv: jax.Array
    gla_s: jax.Array
    tiles: dict


def _run_layers(streams, w):
    xs = [s.x for s in streams]
    hs = [[] for _ in streams]
    convs = [[] for _ in streams]
    s_new = [None for _ in streams]
    for layer in range(DEPTH):
        for n, s in enumerate(streams):
            if layer % 2 == 0:
                xs[n], cb, hl = _rg_block(xs[n], s.group, s.rg_conv, s.rg_h, w, layer,
                                          *s.tiles["rg"])
                convs[n].append(cb)
                hs[n].append(hl)
            else:
                gla_tile = s.tiles["gla"] if s_new[n] is None else s.tiles["gla_in_place"]
                xs[n], s_new[n] = _gla_block(xs[n], s.group, s.gla_s, s_new[n], w, layer,
                                             *gla_tile)
            xs[n] = _xattn(xs[n], s.group, w, layer, s.mem_k, s.mem_v, *s.tiles["xattn"])
        a, b = streams
        xs = list(_mlp(xs[0], a.group, a.tiles["mlp"], xs[1], b.group, b.tiles["mlp"], w, layer))
    results = []
    for n, s in enumerate(streams):
        h_all = jnp.concatenate(hs[n], axis=0).reshape(len(hs[n]), s.group.nb, D_RNN)
        results.append((xs[n], h_all, jnp.concatenate(convs[n], axis=0), s_new[n]))
    return results


def kernel(x_prompt, x_sample, mem_prompt, state_rglru_h, state_rglru_conv, state_gla_S, cache_mem_k, cache_mem_v, norm_mix_g, norm_xa_g, norm_mem_g, norm_mlp_g, final_norm_g, rg_w_in, rg_conv_w, rg_conv_b, rg_w_a, rg_b_a, rg_w_x, rg_b_x, rg_lambda, rg_w_out, gla_w_in, gla_w_a2, gla_b_a, gla_norm_g, gla_w_out, xa_wq, xa_wk, xa_wv, xa_wo, mlp_w1, mlp_w2):
    batch, seq, _ = x_prompt.shape
    dec_batch, dec_seq, _ = x_sample.shape
    n_a = rg_w_in.shape[0]
    n_b = gla_w_in.shape[0]
    assert batch == SUBLANES and dec_seq == SUBLANES

    def rows(p):
        return p.reshape(p.shape[0], 1, p.shape[1])

    gates = jnp.concatenate(
        [jax.vmap(_block_diag_tiles)(rg_w_a), jax.vmap(_block_diag_tiles)(rg_w_x)], axis=-1)
    w = dict(
        norm_mix_g=rows(norm_mix_g), norm_xa_g=rows(norm_xa_g), norm_mlp_g=rows(norm_mlp_g),
        final_norm_g=final_norm_g.reshape(1, D_MODEL),
        rg_w_in=rg_w_in.astype(BF16), rg_conv_w=rg_conv_w, rg_conv_b=rows(rg_conv_b),
        rg_w_gates=gates.astype(BF16), rg_b_a=rows(rg_b_a), rg_b_x=rows(rg_b_x),
        rg_lambda=rows(rg_lambda), rg_w_out=rg_w_out.astype(BF16),
        gla_w_in=gla_w_in.astype(BF16), gla_w_a2=gla_w_a2.astype(BF16),
        gla_b_a=rows(gla_b_a), gla_norm_g=rows(gla_norm_g), gla_w_out=gla_w_out.astype(BF16),
        xa_wq=xa_wq.astype(BF16), xa_wo=xa_wo.astype(BF16),
        mlp_w1=mlp_w1, mlp_w2=mlp_w2,
    )

    mem2d = mem_prompt.reshape(batch * N_MEM, D_MODEL)
    mem_k_prompt, mem_v_prompt, mkb, mvb = _mem_kv(
        mem2d, rows(norm_mem_g), xa_wk.astype(BF16), xa_wv.astype(BF16))
    prompt_tiles = dict(rg=(SUBLANES, 2 * ROW_TILE // SUBLANES), gla=(1, ROW_TILE, GLA_CHUNK),
                        gla_in_place=(1, ROW_TILE, GLA_CHUNK), xattn=(1, 2 * ROW_TILE),
                        mlp=(1, ROW_TILE))
    prompt = _Stream(
        x_prompt, _Group(batch, seq),
        mkb.reshape(DEPTH, batch, N_MEM, D_MODEL), mvb.reshape(DEPTH, batch, N_MEM, D_MODEL),
        jnp.zeros((n_a, batch, 1, D_RNN), F32), jnp.zeros((n_a, batch, CONV_W - 1, D_RNN), F32),
        jnp.zeros((n_b, batch, GLA_HEADS, GLA_HK, GLA_HV), F32), prompt_tiles)

    sample_chunk = GLA_CHUNK if dec_seq % GLA_CHUNK == 0 else dec_seq
    sample_tiles = dict(rg=(32, dec_seq), gla=(8, dec_seq, sample_chunk),
                        gla_in_place=(16, dec_seq, sample_chunk), xattn=(8, dec_seq),
                        mlp=(ROW_TILE // dec_seq, dec_seq))
    sample = _Stream(
        x_sample, _Group(dec_batch, dec_seq), cache_mem_k, cache_mem_v,
        state_rglru_h.reshape(n_a, dec_batch, 1, D_RNN), state_rglru_conv, state_gla_S,
        sample_tiles)

    (y_p, h_p, conv_p, s_p), (y_s, h_s, conv_s, s_s) = _run_layers((prompt, sample), w)
    return (y_p, y_s, mem_k_prompt, mem_v_prompt, h_p, conv_p, s_p, h_s, conv_s, s_s)
```

```python
import functools
from typing import NamedTuple

import jax
import jax.numpy as jnp
from jax import lax
from jax.experimental import pallas as pl
from jax.experimental.pallas import tpu as pltpu

F32 = jnp.float32
BF16 = jnp.bfloat16

D_MODEL = 1024
DEPTH = 4
D_RNN = D_MODEL
RG_BLOCKS = 16
RG_BW = D_RNN // RG_BLOCKS
CONV_W = 4
RG_C = 8.0
GLA_HEADS = 4
GLA_DK = D_MODEL // 2
GLA_DV = D_MODEL
GLA_HK = GLA_DK // GLA_HEADS
GLA_HV = GLA_DV // GLA_HEADS
GLA_RANK = 16
GLA_TAU = 16.0
GLA_CHUNK = 64
N_MEM = 256
XA_HEADS = 4
XA_HD = D_MODEL // XA_HEADS
D_FF = 4 * D_MODEL
EPS = 1e-6

SUBLANES = 8
LANES = 128
MXU_DIM = 256
VMEM_LIMIT_BYTES = 56 * 1024 * 1024
MASKED_SCORE = -1e30
LOG2_E = 1.4426950408889634
ROW_TILE = 512
MLP_STAGE_ELEMS = 512 * 1024


class _Group(NamedTuple):
    nb: int
    t: int

    @property
    def shape(self):
        return (self.nb, self.t, D_MODEL)

    def tile(self, bb, tt):
        return pl.BlockSpec((bb, tt, D_MODEL), lambda b, i: (b, i, 0))

    def grid(self, bb, tt):
        return (self.nb // bb, self.t // tt)


def _params(n_axes):
    return pltpu.CompilerParams(
        dimension_semantics=("arbitrary",) * n_axes,
        vmem_limit_bytes=VMEM_LIMIT_BYTES,
    )


def _layer_spec(shape, layer):
    zeros = (0,) * len(shape)
    return pl.BlockSpec((1,) + tuple(shape), lambda *_: (layer,) + zeros,
                        pipeline_mode=pl.Buffered(1))


def _rmsnorm(x, g):
    ms = jnp.mean(x * x, axis=-1, keepdims=True)
    return x * lax.rsqrt(ms + EPS) * g


def _mm(a, w):
    return jnp.dot(a.astype(BF16), w, preferred_element_type=F32)


def _softplus(z):
    return jnp.maximum(z, 0.0) + jnp.log1p(jnp.exp(-jnp.abs(z)))


def _memkv_kernel(mem_ref, g_ref, wk_ref, wv_ref, k_ref, v_ref, kb_ref, vb_ref, *, nbm):
    mn = _rmsnorm(mem_ref[...], g_ref[0]).astype(BF16)
    k = jnp.dot(mn, wk_ref[0], preferred_element_type=F32)
    v = jnp.dot(mn, wv_ref[0], preferred_element_type=F32)
    k_ref[0] = k.reshape(nbm, N_MEM, XA_HEADS, XA_HD)
    v_ref[0] = v.reshape(nbm, N_MEM, XA_HEADS, XA_HD)
    kb_ref[0] = k.astype(BF16)
    vb_ref[0] = v.astype(BF16)


def _mem_kv(mem2d, norm_g, wk, wv, nbm=2):
    n = mem2d.shape[0]
    nb = n // N_MEM
    tm = nbm * N_MEM
    row = pl.BlockSpec((tm, D_MODEL), lambda l, i: (i, 0))
    per_layer = lambda shape: pl.BlockSpec((1,) + shape, lambda l, i: (l, 0, 0))
    out5 = pl.BlockSpec((1, nbm, N_MEM, XA_HEADS, XA_HD), lambda l, i: (l, i, 0, 0, 0))
    out3 = pl.BlockSpec((1, tm, D_MODEL), lambda l, i: (l, i, 0))
    return pl.pallas_call(
        functools.partial(_memkv_kernel, nbm=nbm),
        grid=(DEPTH, n // tm),
        in_specs=[row, per_layer((1, D_MODEL)), per_layer((D_MODEL, D_MODEL)),
                  per_layer((D_MODEL, D_MODEL))],
        out_specs=[out5, out5, out3, out3],
        out_shape=[jax.ShapeDtypeStruct((DEPTH, nb, N_MEM, XA_HEADS, XA_HD), F32)] * 2
        + [jax.ShapeDtypeStruct((DEPTH, n, D_MODEL), BF16)] * 2,
        compiler_params=_params(2),
        name="mem_kv",
    )(mem2d, norm_g, wk, wv)


def _stream_cast(src, dst, stage, sem, rows):
    n_chunks = src.shape[0] // rows

    def copy(k):
        return pltpu.make_async_copy(src.at[pl.ds(k * rows, rows)], stage.at[k % 2], sem.at[k % 2])

    copy(0).start()
    for k in range(n_chunks):
        if k + 1 < n_chunks:
            copy(k + 1).start()
        copy(k).wait()
        dst[k * rows:(k + 1) * rows, :] = stage[k % 2].astype(BF16)


def _mlp_kernel(xa_ref, xb_ref, g_ref, w1_hbm, w2_hbm, gf_ref, oa_ref, ob_ref,
                w1_ref, w2_ref, stage1_ref, stage2_ref, sem1, sem2, *,
                layer, steps_a, final, f_chunk):
    p = pl.program_id(0)

    @pl.when(p == 0)
    def _():
        _stream_cast(w1_hbm.at[layer], w1_ref, stage1_ref, sem1, stage1_ref.shape[1])
        _stream_cast(w2_hbm.at[layer], w2_ref, stage2_ref, sem2, stage2_ref.shape[1])

    def apply(x_ref, o_ref):
        x = x_ref[...].reshape(-1, D_MODEL)
        xn = _rmsnorm(x, g_ref[0]).astype(BF16)
        acc = x
        for c in range(D_FF // f_chunk):
            cols = slice(c * f_chunk, (c + 1) * f_chunk)
            hid = jnp.dot(xn, w1_ref[:, cols], preferred_element_type=F32)
            act = jnp.square(jnp.maximum(hid, 0.0)).astype(BF16)
            acc = acc + jnp.dot(act, w2_ref[cols, :], preferred_element_type=F32)
        if final:
            acc = _rmsnorm(acc, gf_ref[...])
        o_ref[...] = acc.reshape(o_ref.shape)

    pl.when(p < steps_a)(lambda: apply(xa_ref, oa_ref))
    pl.when(p >= steps_a)(lambda: apply(xb_ref, ob_ref))


def _mlp(xa, group_a, tile_a, xb, group_b, tile_b, w, layer, f_chunk=1024):
    final = layer == DEPTH - 1
    t_steps_a = group_a.grid(*tile_a)[1]
    t_steps_b = group_b.grid(*tile_b)[1]
    steps_a = group_a.grid(*tile_a)[0] * t_steps_a
    steps_b = group_b.grid(*tile_b)[0] * t_steps_b

    def index_a(p):
        q = jnp.minimum(p, steps_a - 1)
        return (q // t_steps_a, q % t_steps_a, 0)

    def index_b(p):
        q = jnp.maximum(p - steps_a, 0)
        return (q // t_steps_b, q % t_steps_b, 0)

    spec_a = pl.BlockSpec(tuple(tile_a) + (D_MODEL,), index_a)
    spec_b = pl.BlockSpec(tuple(tile_b) + (D_MODEL,), index_b)
    hbm = pl.BlockSpec(memory_space=pl.ANY)
    return pl.pallas_call(
        functools.partial(_mlp_kernel, layer=layer, steps_a=steps_a, final=final, f_chunk=f_chunk),
        grid=(steps_a + steps_b,),
        in_specs=[spec_a, spec_b, _layer_spec((1, D_MODEL), layer), hbm, hbm,
                  pl.BlockSpec((1, D_MODEL), lambda p: (0, 0))],
        out_specs=[spec_a, spec_b],
        out_shape=[jax.ShapeDtypeStruct(group_a.shape, F32),
                   jax.ShapeDtypeStruct(group_b.shape, F32)],
        scratch_shapes=[pltpu.VMEM((D_MODEL, D_FF), BF16), pltpu.VMEM((D_FF, D_MODEL), BF16),
                        pltpu.VMEM((2, MLP_STAGE_ELEMS // D_FF, D_FF), F32),
                        pltpu.VMEM((2, MLP_STAGE_ELEMS // D_MODEL, D_MODEL), F32),
                        pltpu.SemaphoreType.DMA((2,)), pltpu.SemaphoreType.DMA((2,))],
        compiler_params=_params(1),
        name="mlp",
    )(xa, xb, w["norm_mlp_g"], w["mlp_w1"], w["mlp_w2"], w["final_norm_g"])


def _xattn_kernel(x_ref, g_ref, wq_ref, wo_ref, k_ref, v_ref, o_ref, att_ref, *, bb, tt, packed_heads):
    n = bb * tt
    x = x_ref[...].reshape(n, D_MODEL)
    xn = _rmsnorm(x, g_ref[0])
    q = _mm(xn, wq_ref[0]) * (XA_HD ** -0.5 * LOG2_E)

    def softmax(s):
        e = jnp.exp2(s - jnp.max(s, axis=-1, keepdims=True))
        return (e / jnp.sum(e, axis=-1, keepdims=True)).astype(BF16)

    nt_dims = (((1,), (1,)), ((), ()))
    if packed_heads:
        rows_b = XA_HEADS * tt
        shape = (bb * rows_b, N_MEM * XA_HEADS)
        q_head = (lax.broadcasted_iota(jnp.int32, shape, 0) % rows_b) // tt
        kv_head = lax.broadcasted_iota(jnp.int32, shape, 1) % XA_HEADS
        same_head = q_head == kv_head
        scores = []
        for b in range(bb):
            qs = jnp.concatenate(
                [q[b * tt:(b + 1) * tt, h * XA_HD:(h + 1) * XA_HD] for h in range(XA_HEADS)],
                axis=0).astype(BF16)
            kf = k_ref[0, b].reshape(N_MEM * XA_HEADS, XA_HD).astype(BF16)
            scores.append(lax.dot_general(qs, kf, nt_dims, preferred_element_type=F32))
        s = jnp.concatenate(scores, axis=0)
        p = softmax(jnp.where(same_head, s, MASKED_SCORE))
        for b in range(bb):
            vf = v_ref[0, b].reshape(N_MEM * XA_HEADS, XA_HD).astype(BF16)
            o = jnp.dot(p[b * rows_b:(b + 1) * rows_b], vf, preferred_element_type=F32)
            for h in range(XA_HEADS):
                att_ref[b * tt:(b + 1) * tt, h * XA_HD:(h + 1) * XA_HD] = o[h * tt:(h + 1) * tt]
    else:
        qb = q.astype(BF16)
        for b in range(bb):
            rows = slice(b * tt, (b + 1) * tt)
            head_cols = [slice(h * XA_HD, (h + 1) * XA_HD) for h in range(XA_HEADS)]
            s = jnp.concatenate(
                [lax.dot_general(qb[rows, cols], k_ref[0, b, :, cols], nt_dims,
                                 preferred_element_type=F32) for cols in head_cols], axis=0)
            p = softmax(s)
            for h, cols in enumerate(head_cols):
                att_ref[rows, cols] = jnp.dot(
                    p[h * tt:(h + 1) * tt], v_ref[0, b, :, cols], preferred_element_type=F32)
    out = x + _mm(att_ref[...], wo_ref[0])
    o_ref[...] = out.reshape(o_ref.shape)


def _xattn(x, group, w, layer, k, v, bb, tt):
    packed_heads = k.ndim == 5
    if packed_heads:
        kv = pl.BlockSpec((1, bb, N_MEM, XA_HEADS, XA_HD), lambda b, i: (layer, b, 0, 0, 0))
    else:
        kv = pl.BlockSpec((1, bb, N_MEM, D_MODEL), lambda b, i: (layer, b, 0, 0))
    return pl.pallas_call(
        functools.partial(_xattn_kernel, bb=bb, tt=tt, packed_heads=packed_heads),
        grid=group.grid(bb, tt),
        in_specs=[group.tile(bb, tt), _layer_spec((1, D_MODEL), layer),
                  _layer_spec((D_MODEL, D_MODEL), layer), _layer_spec((D_MODEL, D_MODEL), layer),
                  kv, kv],
        out_specs=group.tile(bb, tt),
        out_shape=jax.ShapeDtypeStruct(group.shape, F32),
        scratch_shapes=[pltpu.VMEM((bb * tt, D_MODEL), F32)],
        compiler_params=_params(2),
        name="xattn",
    )(x, w["norm_xa_g"], w["xa_wq"], w["xa_wo"], k, v)


def _rg_kernel(x_ref, buf_ref, h0_ref, g_ref, win_ref, cw_ref, cb_ref, wg_ref, ba_ref, bx_ref,
               lam_ref, wout_ref, o_ref, nbuf_ref, hl_ref, u_ref, hc_ref, tm_ref, *, bb, tt, slabs):
    n = bb * tt
    pad = SUBLANES
    n_prev = CONV_W - 1
    assert (bb == SUBLANES) if slabs else (tt == SUBLANES)
    i = pl.program_id(1)

    @pl.when(i == 0)
    def _():
        u_ref[...] = jnp.zeros((bb, pad, D_RNN), F32)
        if slabs:
            for j in range(n_prev):
                u_ref[pad - n_prev + j] = buf_ref[0, :, j, :]
        else:
            u_ref[:, pad - n_prev:pad, :] = buf_ref[0]
        hc_ref[...] = h0_ref[0]

    if slabs:
        x = jnp.concatenate([x_ref[:, t, :] for t in range(tt)], axis=0)
    else:
        x = x_ref[...].reshape(n, D_MODEL)
    xn = _rmsnorm(x, g_ref[0])
    yx = _mm(xn, win_ref[0])
    gate = jax.nn.gelu(yx[:, :D_RNN])
    u = yx[:, D_RNN:]

    def tap(j):
        return cw_ref[0, CONV_W - 1 - j:CONV_W - j, :]

    if slabs:
        steps = [u_ref[pad - n_prev + j] for j in range(n_prev)]
        steps += [u[t * bb:(t + 1) * bb] for t in range(tt)]
        xc_steps = []
        for t in range(tt):
            acc = cb_ref[0] + tap(0) * steps[n_prev + t]
            for j in range(1, CONV_W):
                acc = acc + tap(j) * steps[n_prev + t - j]
            xc_steps.append(acc)
        xc2 = jnp.concatenate(xc_steps, axis=0)
        for j in range(n_prev):
            u_ref[pad - n_prev + j] = steps[tt + j]
            nbuf_ref[0, :, j, :] = steps[tt + j]
    else:
        u3 = u.reshape(bb, tt, D_RNN)
        ext = jnp.concatenate([u_ref[...], u3], axis=1)
        xc = cb_ref[0].reshape(1, 1, D_RNN) + tap(0).reshape(1, 1, D_RNN) * u3
        for j in range(1, CONV_W):
            xc = xc + tap(j).reshape(1, 1, D_RNN) * pltpu.roll(ext, j, 1)[:, pad:, :]
        nbuf_ref[0] = ext[:, pad + tt - n_prev:, :]
        u_ref[...] = ext[:, tt:, :]
        xc2 = xc.reshape(n, D_RNN)

    gates = [_mm(xc2[:, c * MXU_DIM:(c + 1) * MXU_DIM], wg_ref[0, c]) for c in range(D_RNN // MXU_DIM)]
    r = jax.nn.sigmoid(jnp.concatenate([gc[:, :MXU_DIM] for gc in gates], axis=1) + ba_ref[0])
    ig = jax.nn.sigmoid(jnp.concatenate([gc[:, MXU_DIM:] for gc in gates], axis=1) + bx_ref[0])
    log_a = r * ((-RG_C) * _softplus(-lam_ref[0]))
    a = jnp.exp(log_a)
    m2 = -jnp.tanh(log_a) * (a * a + 1.0)
    mult = jnp.where(m2 > 0.0, m2 * lax.rsqrt(m2), 0.0)
    bt = mult * (ig * xc2)

    if slabs:
        carry = hc_ref[:, 0, :]
        h_steps = []
        for t in range(tt):
            rows = slice(t * bb, (t + 1) * bb)
            carry = a[rows] * carry + bt[rows]
            h_steps.append(carry)
        h = jnp.concatenate(h_steps, axis=0)
        hc_ref[:, 0, :] = carry
        hl_ref[0, :, 0, :] = carry
    else:
        ag = a.reshape(bb, tt, D_RNN)
        bg = bt.reshape(bb, tt, D_RNN)
        t_idx = lax.broadcasted_iota(jnp.int32, (bb, tt, D_RNN), 1)
        s = 1
        while s < tt:
            keep = t_idx >= s
            a_s = jnp.where(keep, pltpu.roll(ag, s, 1), 1.0)
            b_s = jnp.where(keep, pltpu.roll(bg, s, 1), 0.0)
            bg = ag * b_s + bg
            ag = ag * a_s
            s *= 2
        h3 = ag * hc_ref[...] + bg
        h = h3.reshape(n, D_RNN)
        hc_ref[...] = h3[:, tt - 1:tt, :]
        hl_ref[0] = h3[:, tt - 1:tt, :]

    out = x + _mm(gate * h, wout_ref[0])
    if slabs:
        tm_ref[...] = out.reshape(tt, bb, D_MODEL)
        for b in range(bb):
            o_ref[b] = tm_ref[:, b, :]
    else:
        o_ref[...] = out.reshape(o_ref.shape)


def _rg_block(x, group, conv_buf, h0, w, layer, bb, tt):
    nb, t = group.nb, group.t
    j = layer // 2
    slabs = bb == SUBLANES and tt > SUBLANES
    tile = group.tile(bb, tt)
    buf_in = pl.BlockSpec((1, bb, CONV_W - 1, D_RNN), lambda b, i: (j, b, 0, 0))
    h_in = pl.BlockSpec((1, bb, 1, D_RNN), lambda b, i: (j, b, 0, 0))
    buf_out = pl.BlockSpec((1, bb, CONV_W - 1, D_RNN), lambda b, i: (0, b, 0, 0))
    h_out = pl.BlockSpec((1, bb, 1, D_RNN), lambda b, i: (0, b, 0, 0))
    vec = _layer_spec((1, D_RNN), j)
    return pl.pallas_call(
        functools.partial(_rg_kernel, bb=bb, tt=tt, slabs=slabs),
        grid=group.grid(bb, tt),
        in_specs=[tile, buf_in, h_in, _layer_spec((1, D_MODEL), layer),
                  _layer_spec((D_MODEL, 2 * D_RNN), j), _layer_spec((CONV_W, D_RNN), j), vec,
                  _layer_spec((D_RNN // MXU_DIM, MXU_DIM, 2 * MXU_DIM), j), vec, vec, vec,
                  _layer_spec((D_RNN, D_MODEL), j)],
        out_specs=[tile, buf_out, h_out],
        out_shape=[jax.ShapeDtypeStruct(group.shape, F32),
                   jax.ShapeDtypeStruct((1, nb, CONV_W - 1, D_RNN), F32),
                   jax.ShapeDtypeStruct((1, nb, 1, D_RNN), F32)],
        scratch_shapes=[pltpu.VMEM((bb, SUBLANES, D_RNN), F32),
                        pltpu.VMEM((bb, 1, D_RNN), F32),
                        pltpu.VMEM((tt, bb, D_MODEL) if slabs else (1, SUBLANES, LANES), F32)],
        compiler_params=_params(2),
        name="rg_block",
    )(x, conv_buf, h0, w["norm_mix_g"], w["rg_w_in"], w["rg_conv_w"], w["rg_conv_b"],
      w["rg_w_gates"], w["rg_b_a"], w["rg_b_x"], w["rg_lambda"], w["rg_w_out"])


def _block_diag_tiles(w):
    per = MXU_DIM // RG_BW
    w4 = w.reshape(RG_BLOCKS // per, per, RG_BW, RG_BW)
    eye = jnp.eye(per, dtype=w.dtype)
    t = w4[:, :, :, None, :] * eye[None, :, None, :, None]
    return t.reshape(RG_BLOCKS // per, MXU_DIM, MXU_DIM)


_GLA_MAIN = 2 * GLA_DK + 2 * GLA_DV
_GLA_IN_PADDED = -(-(_GLA_MAIN + GLA_RANK) // LANES) * LANES


def _gla_kernel(*refs, bb, tt, chunk, slot, aliased):
    (x_ref, s0_ref, g_ref, win_ref, wa2_ref, ba_ref, ng_ref, wout_ref) = refs[:8]
    o_ref, s_ref, oc_ref = refs[9:] if aliased else refs[8:]
    if aliased:
        slot = 0
    n = bb * tt
    n_seg = n // chunk
    assert bb == 1 or tt == chunk
    assert chunk & (chunk - 1) == 0
    shared_state = bb == 1
    i = pl.program_id(1)

    @pl.when(i == 0)
    def _():
        for other in range(s_ref.shape[0]):
            if other != slot:
                s_ref[other] = jnp.zeros(s_ref.shape[1:], F32)
        s_ref[slot] = s0_ref[0]

    x = x_ref[...].reshape(n, D_MODEL)
    xn = _rmsnorm(x, g_ref[0]).astype(BF16)
    proj = jnp.dot(xn, win_ref[0, :, :_GLA_MAIN], preferred_element_type=F32)
    q3 = (proj[:, :GLA_DK] * (GLA_HK ** -0.5)).reshape(n_seg, chunk, GLA_DK)
    k3 = proj[:, GLA_DK:2 * GLA_DK].reshape(n_seg, chunk, GLA_DK)
    v = proj[:, 2 * GLA_DK:2 * GLA_DK + GLA_DV]
    gsilu = jax.nn.silu(proj[:, 2 * GLA_DK + GLA_DV:])
    a_lo = jnp.dot(xn, win_ref[0, :, _GLA_MAIN:_GLA_MAIN + GLA_RANK], preferred_element_type=F32)
    z = _mm(a_lo, wa2_ref[0]) + ba_ref[0]
    nz = -z
    softplus_nz = jnp.maximum(nz, 0.0) + jnp.log(1.0 + jnp.exp(-jnp.abs(nz)))
    la3 = (softplus_nz * (-LOG2_E / GLA_TAU)).reshape(n_seg, chunk, GLA_DK)

    early_cast = (lambda t: t.astype(BF16)) if chunk % (2 * SUBLANES) == 0 else (lambda t: t)
    v = early_cast(v)
    t_idx = lax.broadcasted_iota(jnp.int32, (n_seg, chunk, GLA_HK), 1)
    rg = min(n, MXU_DIM)
    row = lax.broadcasted_iota(jnp.int32, (rg, rg), 0)
    col = lax.broadcasted_iota(jnp.int32, (rg, rg), 1)
    same_chunk_causal = (col <= row) & (col >= (row & (-chunk)))

    for h in range(GLA_HEADS):
        kc = slice(h * GLA_HK, (h + 1) * GLA_HK)
        vc = slice(h * GLA_HV, (h + 1) * GLA_HV)
        bcum = la3[:, :, kc]
        s = 1
        while s < chunk:
            bcum = bcum + jnp.where(t_idx >= s, pltpu.roll(bcum, s, 1), 0.0)
            s *= 2
        gl = bcum[:, chunk - 1:chunk, :]
        q_in = early_cast((q3[:, :, kc] * jnp.exp2(bcum)).reshape(n, GLA_HK))
        k_dec = k3[:, :, kc] * jnp.exp2(-bcum)
        k_in = early_cast(k_dec.reshape(n, GLA_HK))
        k_end = early_cast((k_dec * jnp.exp2(gl)).reshape(n, GLA_HK))

        for r in range(n // rg):
            rows = slice(r * rg, (r + 1) * rg)
            att = lax.dot_general(q_in[rows].astype(BF16), k_in[rows].astype(BF16),
                                  (((1,), (1,)), ((), ())), preferred_element_type=F32)
            att = jnp.where(same_chunk_causal, att, 0.0).astype(BF16)
            oc_ref[rows, vc] = jnp.dot(att, v[rows, vc].astype(BF16), preferred_element_type=F32)

        state = None
        for c in range(n_seg):
            rows = slice(c * chunk, (c + 1) * chunk)
            sb = 0 if shared_state else c
            if state is None or not shared_state:
                state = s_ref[slot, sb, h]
            oc_ref[rows, vc] = oc_ref[rows, vc] + jnp.dot(
                q_in[rows].astype(BF16), state.astype(BF16), preferred_element_type=F32)
            eg = jnp.exp2(jnp.broadcast_to(gl[c], (GLA_HK, GLA_HK))).T
            eg = jnp.concatenate([eg] * (GLA_HV // GLA_HK), axis=1)
            state = eg * state + lax.dot_general(
                k_end[rows].astype(BF16), v[rows, vc].astype(BF16),
                (((0,), (0,)), ((), ())), preferred_element_type=F32)
            if not shared_state or c == n_seg - 1:
                s_ref[slot, sb, h] = state

    o = oc_ref[...]
    heads = []
    for h in range(GLA_HEADS):
        heads.append(_rmsnorm(o[:, h * GLA_HV:(h + 1) * GLA_HV], ng_ref[0]))
    on = jnp.concatenate(heads, axis=1)
    out = x + _mm(on * gsilu, wout_ref[0])
    o_ref[...] = out.reshape(o_ref.shape)


def _gla_block(x, group, s0, s_new, w, layer, bb, tt, chunk):
    n = bb * tt
    j = layer // 2
    tile = group.tile(bb, tt)
    state = pl.BlockSpec((1, bb, GLA_HEADS, GLA_HK, GLA_HV), lambda b, i: (j, b, 0, 0, 0))
    in_specs = [tile, state, _layer_spec((1, D_MODEL), layer),
                _layer_spec((D_MODEL, _GLA_IN_PADDED), j), _layer_spec((GLA_RANK, GLA_DK), j),
                _layer_spec((1, GLA_DK), j), _layer_spec((1, GLA_HV), j),
                _layer_spec((GLA_DV, D_MODEL), j)]
    args = [x, s0, w["norm_mix_g"], w["gla_w_in"], w["gla_w_a2"], w["gla_b_a"],
            w["gla_norm_g"], w["gla_w_out"]]
    if s_new is None:
        aliases = {}
        state_out = pl.BlockSpec((s0.shape[0], bb, GLA_HEADS, GLA_HK, GLA_HV),
                                 lambda b, i: (0, b, 0, 0, 0))
    else:
        in_specs.append(pl.BlockSpec(memory_space=pl.ANY))
        args.append(s_new)
        aliases = {len(args) - 1: 1}
        state_out = state
    return pl.pallas_call(
        functools.partial(_gla_kernel, bb=bb, tt=tt, chunk=chunk, slot=j,
                          aliased=s_new is not None),
        grid=group.grid(bb, tt),
        in_specs=in_specs,
        out_specs=[tile, state_out],
        out_shape=[jax.ShapeDtypeStruct(group.shape, F32), jax.ShapeDtypeStruct(s0.shape, F32)],
        input_output_aliases=aliases,
        scratch_shapes=[pltpu.VMEM((n, GLA_DV), F32)],
        compiler_params=_params(2),
        name="gla_block",
    )(*args)


class _Stream(NamedTuple):
    x: jax.Array
    group: _Group
    mem_k: jax.Array
    mem_v: jax.Array
    rg_h: jax.Array
    rg_conv: jax.Array
    gla_s: jax.Array
    tiles: dict


def _run_layers(streams, w):
    xs = [s.x for s in streams]
    hs = [[] for _ in streams]
    convs = [[] for _ in streams]
    s_new = [None for _ in streams]
    for layer in range(DEPTH):
        for n, s in enumerate(streams):
            if layer % 2 == 0:
                xs[n], cb, hl = _rg_block(xs[n], s.group, s.rg_conv, s.rg_h, w, layer,
                                          *s.tiles["rg"])
                convs[n].append(cb)
                hs[n].append(hl)
            else:
                gla_tile = s.tiles["gla"] if s_new[n] is None else s.tiles["gla_in_place"]
                xs[n], s_new[n] = _gla_block(xs[n], s.group, s.gla_s, s_new[n], w, layer,
                                             *gla_tile)
            xs[n] = _xattn(xs[n], s.group, w, layer, s.mem_k, s.mem_v, *s.tiles["xattn"])
        a, b = streams
        xs = list(_mlp(xs[0], a.group, a.tiles["mlp"], xs[1], b.group, b.tiles["mlp"], w, layer))
    results = []
    for n, s in enumerate(streams):
        h_all = jnp.concatenate(hs[n], axis=0).reshape(len(hs[n]), s.group.nb, D_RNN)
        results.append((xs[n], h_all, jnp.concatenate(convs[n], axis=0), s_new[n]))
    return results


def kernel(x_prompt, x_sample, mem_prompt, state_rglru_h, state_rglru_conv, state_gla_S, cache_mem_k, cache_mem_v, norm_mix_g, norm_xa_g, norm_mem_g, norm_mlp_g, final_norm_g, rg_w_in, rg_conv_w, rg_conv_b, rg_w_a, rg_b_a, rg_w_x, rg_b_x, rg_lambda, rg_w_out, gla_w_in, gla_w_a2, gla_b_a, gla_norm_g, gla_w_out, xa_wq, xa_wk, xa_wv, xa_wo, mlp_w1, mlp_w2):
    batch, seq, _ = x_prompt.shape
    dec_batch, dec_seq, _ = x_sample.shape
    n_a = rg_w_in.shape[0]
    n_b = gla_w_in.shape[0]
    assert batch == SUBLANES and dec_seq == SUBLANES

    def rows(p):
        return p.reshape(p.shape[0], 1, p.shape[1])

    gates = jnp.concatenate(
        [jax.vmap(_block_diag_tiles)(rg_w_a), jax.vmap(_block_diag_tiles)(rg_w_x)], axis=-1)
    w = dict(
        norm_mix_g=rows(norm_mix_g), norm_xa_g=rows(norm_xa_g), norm_mlp_g=rows(norm_mlp_g),
        final_norm_g=final_norm_g.reshape(1, D_MODEL),
        rg_w_in=rg_w_in.astype(BF16), rg_conv_w=rg_conv_w, rg_conv_b=rows(rg_conv_b),
        rg_w_gates=gates.astype(BF16), rg_b_a=rows(rg_b_a), rg_b_x=rows(rg_b_x),
        rg_lambda=rows(rg_lambda), rg_w_out=rg_w_out.astype(BF16),
        gla_w_in=jnp.pad(gla_w_in.astype(BF16),
                         ((0, 0), (0, 0), (0, _GLA_IN_PADDED - gla_w_in.shape[-1]))),
        gla_w_a2=gla_w_a2.astype(BF16),
        gla_b_a=rows(gla_b_a), gla_norm_g=rows(gla_norm_g), gla_w_out=gla_w_out.astype(BF16),
        xa_wq=xa_wq.astype(BF16), xa_wo=xa_wo.astype(BF16),
        mlp_w1=mlp_w1, mlp_w2=mlp_w2,
    )

    mem2d = mem_prompt.reshape(batch * N_MEM, D_MODEL)
    mem_k_prompt, mem_v_prompt, mkb, mvb = _mem_kv(
        mem2d, rows(norm_mem_g), xa_wk.astype(BF16), xa_wv.astype(BF16))
    prompt_tiles = dict(rg=(SUBLANES, 2 * ROW_TILE // SUBLANES), gla=(1, ROW_TILE, GLA_CHUNK),
                        gla_in_place=(1, ROW_TILE, GLA_CHUNK), xattn=(1, 2 * ROW_TILE),
                        mlp=(1, ROW_TILE))
    prompt = _Stream(
        x_prompt, _Group(batch, seq),
        mkb.reshape(DEPTH, batch, N_MEM, D_MODEL), mvb.reshape(DEPTH, batch, N_MEM, D_MODEL),
        jnp.zeros((n_a, batch, 1, D_RNN), F32), jnp.zeros((n_a, batch, CONV_W - 1, D_RNN), F32),
        jnp.zeros((n_b, batch, GLA_HEADS, GLA_HK, GLA_HV), F32), prompt_tiles)

    sample_chunk = GLA_CHUNK if dec_seq % GLA_CHUNK == 0 else dec_seq
    sample_tiles = dict(rg=(32, dec_seq), gla=(8, dec_seq, sample_chunk),
                        gla_in_place=(16, dec_seq, sample_chunk), xattn=(8, dec_seq),
                        mlp=(ROW_TILE // dec_seq, dec_seq))
    sample = _Stream(
        x_sample, _Group(dec_batch, dec_seq), cache_mem_k, cache_mem_v,
        state_rglru_h.reshape(n_a, dec_batch, 1, D_RNN), state_rglru_conv, state_gla_S,
        sample_tiles)

    (y_p, h_p, conv_p, s_p), (y_s, h_s, conv_s, s_s) = _run_layers((prompt, sample), w)
    return (y_p, y_s, mem_k_prompt, mem_v_prompt, h_p, conv_p, s_p, h_s, conv_s, s_s)
```
